```python
import jax, jax.numpy as jnp
from jax import lax
import numpy as np

D_MODEL = 1024
BATCH = 8
SEQ = 4096
DEPTH = 1

CTX_LEN = 256
GRID_W = 64

CONV_DIM = D_MODEL
CONV_K = 31
SSM_EXPAND = 2
SSM_INNER = SSM_EXPAND * D_MODEL
SSM_HEAD_DIM = 64
SSM_HEADS = SSM_INNER // SSM_HEAD_DIM
SSM_GROUPS = 8
SSM_HPG = SSM_HEADS // SSM_GROUPS
SSM_STATE = 128
SSM_CONV_K = 5
SSM_CHUNK = 128
SSM_GN = SSM_GROUPS * SSM_STATE
XBC_DIM = SSM_INNER + 2 * SSM_GN
D_FF = 4 * D_MODEL
N_BRANCH = 2
COL_XBC = 0
COL_DT = COL_XBC + XBC_DIM
COL_Z = COL_DT + 2 * SSM_HEADS
COL_GLU = COL_Z + SSM_INNER
COL_GATE = COL_GLU + 2 * CONV_DIM
PROJ_DIM = COL_GATE + N_BRANCH * D_MODEL
EPS = 1e-6

kernel_name = "hybrid_conformer_ssd_prefix_dit_block"

F32 = jnp.float32


def rmsnorm(x, w):
    xf = x.astype(F32)
    y = xf * lax.rsqrt(jnp.mean(xf * xf, axis=-1, keepdims=True) + EPS)
    return (y * w.astype(F32)).astype(x.dtype)


def layernorm(x, w, b):
    xf = x.astype(F32)
    mu = jnp.mean(xf, axis=-1, keepdims=True)
    xc = xf - mu
    var = jnp.mean(xc * xc, axis=-1, keepdims=True)
    return (xc * lax.rsqrt(var + EPS) * w.astype(F32) + b.astype(F32)).astype(x.dtype)


def adaln(cond, w, b):
    m = (jax.nn.silu(cond) @ w + b)[:, None, :]
    return jnp.split(m, 6, axis=-1)


def modulate(h, shift, scale):
    return h * (1 + scale) + shift


def dwconv(x, w, b):
    y = lax.conv_general_dilated(x, w[:, None, :].astype(x.dtype), window_strides=(1,),
                                 padding='SAME', dimension_numbers=('NWC', 'WIO', 'NWC'),
                                 feature_group_count=x.shape[-1])
    return y + b


def flip(t):
    return jnp.flip(t, axis=1)


def ssd_scan(xs, dt, a, bm, cm, h0):
    b, l, h, p = xs.shape
    g, n = bm.shape[2], bm.shape[3]
    r = h // g
    q = SSM_CHUNK
    c = l // q
    x5 = xs.astype(F32).reshape(b, c, q, g, r, p)
    dts = dt.astype(F32).reshape(b, c, q, g, r)
    bs = bm.astype(F32).reshape(b, c, q, g, n)
    cs_ = cm.astype(F32).reshape(b, c, q, g, n)
    cum = jnp.cumsum(dts * a.astype(F32).reshape(g, r), axis=2)
    xdt = x5 * dts[..., None]
    cum_t = jnp.moveaxis(cum, 2, -1)
    seg = cum_t[..., :, None] - cum_t[..., None, :]
    lower = jnp.tril(jnp.ones((q, q), dtype=bool))
    decay = jnp.exp(jnp.where(lower, seg, -jnp.inf))
    cb = jnp.einsum('bcqgn,bckgn->bcgqk', cs_, bs)
    y_diag = jnp.einsum('bcgrqk,bckgrp->bcqgrp', cb[:, :, :, None] * decay, xdt)
    decay_to_end = jnp.exp(cum[:, :, -1:] - cum)
    states = jnp.einsum('bcqgn,bcqgr,bcqgrp->bcgrpn', bs, decay_to_end, xdt)
    chunk_decay = jnp.exp(cum[:, :, -1])

    def step(hc, inp):
        s, d = inp
        return hc * d[..., None, None] + s, hc

    h_final, h_in = lax.scan(step, h0.astype(F32),
                             (jnp.moveaxis(states, 1, 0), jnp.moveaxis(chunk_decay, 1, 0)))
    h_in = jnp.moveaxis(h_in, 0, 1)
    y_off = jnp.einsum('bcqgn,bcgrpn,bcqgr->bcqgrp', cs_, h_in, jnp.exp(cum))
    return (y_diag + y_off).reshape(b, l, h, p), h_final


def ssd_final_state(xs, dt, a, bm):
    b, l, h, p = xs.shape
    g = bm.shape[2]
    r = h // g
    dtf = dt.astype(F32).reshape(b, l, g, r)
    cum = jnp.cumsum(dtf * a.astype(F32).reshape(g, r), axis=1)
    wgt = jnp.exp(cum[:, -1:] - cum) * dtf
    return jnp.einsum('blgn,blgr,blgrp->bgrpn', bm.astype(F32), wgt,
                      xs.astype(F32).reshape(b, l, g, r, p))


def ssd_inputs(proj, lp):
    b, l, _ = proj.shape
    xbc = jax.nn.silu(dwconv(proj[..., COL_XBC:COL_DT], lp['ssm_conv_w'], lp['ssm_conv_b']))
    xs = xbc[..., :SSM_INNER].reshape(b, l, SSM_HEADS, SSM_HEAD_DIM)
    bm = xbc[..., SSM_INNER:SSM_INNER + SSM_GN].reshape(b, l, SSM_GROUPS, SSM_STATE)
    cm = xbc[..., SSM_INNER + SSM_GN:].reshape(b, l, SSM_GROUPS, SSM_STATE)
    dt = jax.nn.softplus(proj[..., COL_DT:COL_Z].reshape(b, l, 2, SSM_HEADS).astype(F32)
                         + lp['ssm_dt_bias'].astype(F32))
    return xs, bm, cm, dt[:, :, 0], dt[:, :, 1]


def token_mixer(proj, n_seg, seg_len, h_f, h_b, a, lp):
    bsz, l, _ = proj.shape
    glu = proj[..., COL_GLU:COL_GATE]
    u = glu[..., :CONV_DIM] * jax.nn.sigmoid(glu[..., CONV_DIM:])
    u = dwconv(u.reshape(bsz * n_seg, seg_len, CONV_DIM), lp['conv_dw_w'], lp['conv_dw_b'])
    u = jax.nn.silu(layernorm(u.reshape(bsz, l, CONV_DIM), lp['conv_ln_w'], lp['conv_ln_b']))
    u_conv = u @ lp['w_conv_out'] + lp['b_conv_out']
    xs, bm, cm, dt_f, dt_b = ssd_inputs(proj, lp)
    y_f, hf_out = ssd_scan(xs, dt_f, a[0], bm, cm, h_f)
    y_b, hb_out = ssd_scan(flip(xs), flip(dt_b), a[1], flip(bm), flip(cm), h_b)
    y = y_f + flip(y_b) + lp['ssm_d'].astype(F32)[:, None] * xs.astype(F32)
    y = y.reshape(bsz, l, SSM_INNER).astype(proj.dtype)
    y = rmsnorm(y * jax.nn.silu(proj[..., COL_Z:COL_GLU]), lp['ssm_norm_w'])
    u_ssd = y @ lp['w_ssm_out']
    g_conv, g_ssd = jnp.split(jax.nn.sigmoid(proj[..., COL_GATE:]), 2, axis=-1)
    return (g_conv * u_conv + g_ssd * u_ssd) @ lp['w_o'], hf_out, hb_out


def sq_relu_mlp(h, w1, w2):
    return jnp.square(jax.nn.relu(h @ w1)) @ w2


def setup_inputs(seed: int = 0) -> dict:
    key = jax.random.key(seed)
    ks = jax.random.split(key, 32)
    nrm = jax.random.normal
    D, L = D_MODEL, DEPTH
    dt0 = jnp.exp(jax.random.uniform(ks[16], (L, 2, SSM_HEADS), minval=np.log(1e-3), maxval=np.log(1e-1)))
    return {
        'x': nrm(ks[0], (BATCH, SEQ, D), F32),
        'c': nrm(ks[1], (BATCH, D), F32),
        'ctx': nrm(ks[2], (BATCH, CTX_LEN, D), F32),
        'c_ctx': nrm(ks[3], (D,), F32),
        'w_ada': nrm(ks[4], (L, D, 6 * D), F32) * D ** -0.5,
        'b_ada': nrm(ks[5], (L, 6 * D), F32) * 0.02,
        'norm1_w': 1.0 + 0.02 * nrm(ks[6], (L, D), F32),
        'norm2_w': 1.0 + 0.02 * nrm(ks[7], (L, D), F32),
        'w_in': nrm(ks[8], (L, D, PROJ_DIM), F32) * D ** -0.5,
        'conv_dw_w': nrm(ks[9], (L, CONV_K, CONV_DIM), F32) * CONV_K ** -0.5,
        'conv_dw_b': nrm(ks[10], (L, CONV_DIM), F32) * 0.02,
        'conv_ln_w': 1.0 + 0.02 * nrm(ks[11], (L, CONV_DIM), F32),
        'conv_ln_b': nrm(ks[12], (L, CONV_DIM), F32) * 0.02,
        'w_conv_out': nrm(ks[13], (L, CONV_DIM, D), F32) * CONV_DIM ** -0.5,
        'b_conv_out': nrm(ks[14], (L, D), F32) * 0.02,
        'ssm_conv_w': nrm(ks[15], (L, SSM_CONV_K, XBC_DIM), F32) * SSM_CONV_K ** -0.5,
        'ssm_conv_b': nrm(ks[17], (L, XBC_DIM), F32) * 0.02,
        'ssm_dt_bias': (dt0 + jnp.log(-jnp.expm1(-dt0))).astype(F32),
        'ssm_a_log': jnp.log(jax.random.uniform(ks[18], (L, 2, SSM_HEADS), minval=1.0, maxval=16.0)).astype(F32),
        'ssm_d': 1.0 + 0.02 * nrm(ks[19], (L, SSM_HEADS), F32),
        'ssm_norm_w': 1.0 + 0.02 * nrm(ks[20], (L, SSM_INNER), F32),
        'w_ssm_out': nrm(ks[21], (L, SSM_INNER, D), F32) * SSM_INNER ** -0.5,
        'w_o': nrm(ks[22], (L, D, D), F32) * D ** -0.5,
        'w_mlp1': nrm(ks[23], (L, D, D_FF), F32) * D ** -0.5,
        'w_mlp2': nrm(ks[24], (L, D_FF, D), F32) * D_FF ** -0.5,
        'final_norm_w': 1.0 + 0.02 * nrm(ks[25], (D,), F32),
    }


def reference(x, c, ctx, c_ctx, w_ada, b_ada, norm1_w, norm2_w, w_in, conv_dw_w, conv_dw_b,
              conv_ln_w, conv_ln_b, w_conv_out, b_conv_out, ssm_conv_w, ssm_conv_b,
              ssm_dt_bias, ssm_a_log, ssm_d, ssm_norm_w, w_ssm_out, w_o, w_mlp1, w_mlp2,
              final_norm_w):
    bsz, seq, _ = x.shape
    rows = seq // GRID_W
    ctx_len = ctx.shape[1]
    for i in range(DEPTH):
        last = i == DEPTH - 1
        lp = {
            'conv_dw_w': conv_dw_w[i], 'conv_dw_b': conv_dw_b[i],
            'conv_ln_w': conv_ln_w[i], 'conv_ln_b': conv_ln_b[i],
            'w_conv_out': w_conv_out[i], 'b_conv_out': b_conv_out[i],
            'ssm_conv_w': ssm_conv_w[i], 'ssm_conv_b': ssm_conv_b[i],
            'ssm_dt_bias': ssm_dt_bias[i], 'ssm_d': ssm_d[i], 'ssm_norm_w': ssm_norm_w[i],
            'w_ssm_out': w_ssm_out[i], 'w_o': w_o[i],
        }
        a = -jnp.exp(ssm_a_log[i].astype(F32))
        mod = adaln(c, w_ada[i], b_ada[i])
        mod_c = adaln(c_ctx[None, :], w_ada[i], b_ada[i])
        hc = modulate(rmsnorm(ctx, norm1_w[i]), mod_c[0], mod_c[1])
        if last:
            pc = hc @ w_in[i][:, :COL_Z]
            xs_c, b_c, _, dtf_c, dtb_c = ssd_inputs(pc, lp)
            h_f = ssd_final_state(xs_c, dtf_c, a[0], b_c)
            h_b = ssd_final_state(flip(xs_c), flip(dtb_c), a[1], flip(b_c))
        else:
            zero_state = jnp.zeros((bsz, SSM_GROUPS, SSM_HPG, SSM_HEAD_DIM, SSM_STATE), F32)
            mix_c, h_f, h_b = token_mixer(hc @ w_in[i], 1, ctx_len, zero_state, zero_state, a, lp)
            ctx = ctx + mod_c[2] * mix_c
            hc2 = modulate(rmsnorm(ctx, norm2_w[i]), mod_c[3], mod_c[4])
            ctx = ctx + mod_c[5] * sq_relu_mlp(hc2, w_mlp1[i], w_mlp2[i])
        h = modulate(rmsnorm(x, norm1_w[i]), mod[0], mod[1])
        mix, _, _ = token_mixer(h @ w_in[i], rows, GRID_W, h_f, h_b, a, lp)
        x = x + mod[2] * mix
        h2 = modulate(rmsnorm(x, norm2_w[i]), mod[3], mod[4])
        x = x + mod[5] * sq_relu_mlp(h2, w_mlp1[i], w_mlp2[i])
    return rmsnorm(x, final_norm_w)
```

```python
import functools

import jax
import jax.numpy as jnp
from jax import lax
from jax.experimental import pallas as pl
from jax.experimental.pallas import tpu as pltpu

F32 = jnp.float32
BF16 = jnp.bfloat16

D = 1024
SEQ = 4096
CTX = 256
GRID_W = 64
CONV_K = 31
INNER = 2048
HEADS = 32
HEAD_DIM = 64
GROUPS = 8
STATE = 128
GN = GROUPS * STATE
SSM_K = 5
CHUNK = 128
XBC = INNER + 2 * GN
D_FF = 4 * D
COL_DT = XBC
COL_Z = COL_DT + 2 * HEADS
COL_GLU = COL_Z + INNER
COL_GATE = COL_GLU + 2 * D
EPS = 1e-6

LANES = 128
HALO = 16
VMEM_LIMIT = 56 * 1024 * 1024

TM_IN = 512
CN_IN = 512
TM_CONV = 512
TM_MERGE = 256
TM_MLP = 512
FF_CHUNK = 1024


def _dot(a, b):
    return jnp.dot(a, b, preferred_element_type=F32)


def _dot_nt(a, b):
    return lax.dot_general(a, b, (((1,), (1,)), ((), ())), preferred_element_type=F32)


def _dot_tn(a, b):
    return lax.dot_general(a, b, (((0,), (0,)), ((), ())), preferred_element_type=F32)


def _sigmoid(v):
    return 1.0 / (1.0 + jnp.exp(-v))


def _silu(v):
    return v * _sigmoid(v)


def _norm_mod(x, nw, shift, scale):
    ms = jnp.mean(x * x, axis=-1, keepdims=True)
    y = x * lax.rsqrt(ms + EPS) * nw
    return y * (1.0 + scale) + shift


def _split3(v):
    hi = v.astype(BF16)
    r1 = v - hi.astype(F32)
    mid = r1.astype(BF16)
    lo = (r1 - mid.astype(F32)).astype(BF16)
    return hi, mid, lo


def _dt_lanes(h, wdt_ref, dtb_ref, alog_ref):
    w = wdt_ref[...]
    w_hi = w.astype(BF16)
    w_lo = (w - w_hi.astype(F32)).astype(BF16)
    h_hi = h.astype(BF16)
    h_lo = (h - h_hi.astype(F32)).astype(BF16)
    raw = _dot(h_hi, w_hi) + _dot(h_hi, w_lo) + _dot(h_lo, w_hi)
    v = raw + dtb_ref[...]
    dt = jnp.maximum(v, 0.0) + jnp.log1p(jnp.exp(-jnp.abs(v)))
    lane = lax.broadcasted_iota(jnp.int32, (1, LANES), 1)
    mult = jnp.where((lane % 64) < HEADS, 1.0, -jnp.exp(alog_ref[...]))
    return dt * mult


def _ada_kernel(cond_ref, w_ref, b_ref, o_ref):
    c = cond_ref[...]
    s = _silu(c)
    o_ref[...] = _dot(s.astype(BF16), w_ref[...].astype(BF16)) + b_ref[...]


def _ada(cond, w, b):
    n = w.shape[1]
    tn = 1024
    return pl.pallas_call(
        _ada_kernel,
        grid=(n // tn,),
        in_specs=[
            pl.BlockSpec((16, D), lambda j: (0, 0)),
            pl.BlockSpec((D, tn), lambda j: (0, j)),
            pl.BlockSpec((1, tn), lambda j: (0, j)),
        ],
        out_specs=pl.BlockSpec((16, tn), lambda j: (0, j)),
        out_shape=jax.ShapeDtypeStruct((16, n), F32),
        name="ada",
    )(cond, w, b)


def _conv5_silu(p_scr, rows, cw, cb):
    acc = cb
    for k in range(SSM_K):
        acc = acc + p_scr[pl.ds(HALO - SSM_K // 2 + k, rows), :] * cw[k:k + 1, :]
    return _silu(acc)


def _ssd_in_kernel(xp_ref, x_ref, xn_ref, nw_ref, sh_ref, sc_ref, w_ref, wdt_ref, dtb_ref,
                   alog_ref, cw_ref, cb_ref, xbc_ref, dt_ref, h_scr, p_scr):
    i = pl.program_id(0)
    tpb = SEQ // TM_IN
    first = (i % tpb) == 0
    last = (i % tpb) == tpb - 1
    nw, sh, sc = nw_ref[...], sh_ref[...], sc_ref[...]
    hm = _norm_mod(x_ref[...], nw, sh, sc)
    h_scr[pl.ds(0, HALO), :] = _norm_mod(xp_ref[...], nw, sh, sc).astype(BF16)
    h_scr[pl.ds(HALO, TM_IN), :] = hm.astype(BF16)
    h_scr[pl.ds(HALO + TM_IN, HALO), :] = _norm_mod(xn_ref[...], nw, sh, sc).astype(BF16)

    val = _dt_lanes(hm, wdt_ref, dtb_ref, alog_ref)
    lane = lax.broadcasted_iota(jnp.int32, (1, LANES), 1)
    dt_ref[0] = jnp.where(lane < 64, val, 0.0)
    dt_ref[1] = jnp.where(lane < 64, pltpu.roll(val, 64, axis=1), 0.0)

    for c in range(XBC // CN_IN):
        cols = pl.ds(c * CN_IN, CN_IN)
        p_scr[...] = _dot(h_scr[...], w_ref[:, cols])

        @pl.when(first)
        def _():
            p_scr[pl.ds(0, HALO), :] = jnp.zeros((HALO, CN_IN), F32)

        @pl.when(last)
        def _():
            p_scr[pl.ds(HALO + TM_IN, HALO), :] = jnp.zeros((HALO, CN_IN), F32)

        xbc_ref[:, cols] = _conv5_silu(p_scr, TM_IN, cw_ref[:, cols], cb_ref[:, cols]).astype(BF16)


def _ssd_in(x2, mod3, nw, w_xbc, wdt, dtb, alog, cw, cb):
    n = x2.shape[0]
    nt = n // TM_IN
    tpb = SEQ // TM_IN
    hb = TM_IN // HALO
    nhb = n // HALO
    const = lambda i: (0, 0)
    return pl.pallas_call(
        _ssd_in_kernel,
        grid=(nt,),
        in_specs=[
            pl.BlockSpec((HALO, D), lambda i: (jnp.maximum(i * hb - 1, 0), 0)),
            pl.BlockSpec((TM_IN, D), lambda i: (i, 0)),
            pl.BlockSpec((HALO, D), lambda i: (jnp.minimum((i + 1) * hb, nhb - 1), 0)),
            pl.BlockSpec((1, D), const),
            pl.BlockSpec((None, 1, D), lambda i: (i // tpb, 0, 0)),
            pl.BlockSpec((None, 1, D), lambda i: (i // tpb, 0, 1)),
            pl.BlockSpec((D, XBC), const),
            pl.BlockSpec((D, LANES), const),
            pl.BlockSpec((1, LANES), const),
            pl.BlockSpec((1, LANES), const),
            pl.BlockSpec((SSM_K, XBC), const),
            pl.BlockSpec((1, XBC), const),
        ],
        out_specs=[
            pl.BlockSpec((TM_IN, XBC), lambda i: (i, 0)),
            pl.BlockSpec((2, TM_IN, LANES), lambda i: (0, i, 0)),
        ],
        out_shape=[
            jax.ShapeDtypeStruct((n, XBC), BF16),
            jax.ShapeDtypeStruct((2, n, LANES), F32),
        ],
        scratch_shapes=[
            pltpu.VMEM((TM_IN + 2 * HALO, D), BF16),
            pltpu.VMEM((TM_IN + 2 * HALO, CN_IN), F32),
        ],
        compiler_params=pltpu.CompilerParams(vmem_limit_bytes=VMEM_LIMIT),
        name="ssd_in",
    )(x2, x2, x2, nw, mod3, mod3, w_xbc, wdt, dtb, alog, cw, cb)


CTX_COLS = INNER + GN


def _ctx_kernel(c_ref, nw_ref, sh_ref, sc_ref, w_ref, wdt_ref, dtb_ref, alog_ref, cw_ref, cb_ref,
                h0_ref, p_scr, xb_scr):
    hc = _norm_mod(c_ref[...], nw_ref[...], sh_ref[...], sc_ref[...])
    hb = hc.astype(BF16)
    p_scr[pl.ds(0, HALO), :] = jnp.zeros((HALO, CN_IN), F32)
    p_scr[pl.ds(HALO + CTX, HALO), :] = jnp.zeros((HALO, CN_IN), F32)
    for c in range(CTX_COLS // CN_IN):
        cols = pl.ds(c * CN_IN, CN_IN)
        p_scr[pl.ds(HALO, CTX), :] = _dot(hb, w_ref[:, cols])
        xb_scr[:, cols] = _conv5_silu(p_scr, CTX, cw_ref[:, cols], cb_ref[:, cols]).astype(BF16)

    val = _dt_lanes(hc, wdt_ref, dtb_ref, alog_ref)
    row = lax.broadcasted_iota(jnp.int32, (CTX, CTX), 0)
    col = lax.broadcasted_iota(jnp.int32, (CTX, CTX), 1)
    tri = (row >= col).astype(BF16)
    v_hi, v_mid, v_lo = _split3(val)
    cum = _dot(tri, v_hi) + _dot(tri, v_mid) + _dot(tri, v_lo)
    tot = cum[CTX - 1:CTX, :]
    dts = pltpu.roll(val, HEADS, axis=1)
    lane = lax.broadcasted_iota(jnp.int32, (1, LANES), 1)
    wgt = jnp.where(lane < 64, jnp.exp(tot - cum), jnp.exp(cum - val)) * dts
    lane2 = lax.broadcasted_iota(jnp.int32, (CTX, LANES), 1)
    for d in range(2):
        base = d * 64 + HEADS
        for pair in range(HEADS // 2):
            g = pair // 2
            la = base + 2 * pair
            wcol = jnp.where(lane2 < HEAD_DIM, wgt[:, la:la + 1], wgt[:, la + 1:la + 2])
            xs = (xb_scr[:, pl.ds(pair * LANES, LANES)].astype(F32) * wcol).astype(BF16)
            bg = xb_scr[:, pl.ds(INNER + g * STATE, STATE)]
            h0_ref[d, pl.ds(pair * LANES, LANES), :] = _dot_tn(xs, bg)


def _ctx(ctx, mod3, nw, w_xbc, wdt, dtb, alog, cw, cb):
    b = ctx.shape[0]
    const = lambda i: (0, 0)
    return pl.pallas_call(
        _ctx_kernel,
        grid=(b,),
        in_specs=[
            pl.BlockSpec((None, CTX, D), lambda i: (i, 0, 0)),
            pl.BlockSpec((1, D), const),
            pl.BlockSpec((None, 1, D), lambda i: (8, 0, 0)),
            pl.BlockSpec((None, 1, D), lambda i: (8, 0, 1)),
            pl.BlockSpec((D, CTX_COLS), const),
            pl.BlockSpec((D, LANES), const),
            pl.BlockSpec((1, LANES), const),
            pl.BlockSpec((1, LANES), const),
            pl.BlockSpec((SSM_K, CTX_COLS), const),
            pl.BlockSpec((1, CTX_COLS), const),
        ],
        out_specs=pl.BlockSpec((2, None, INNER, STATE), lambda i: (0, i, 0, 0)),
        out_shape=jax.ShapeDtypeStruct((2, b, INNER, STATE), F32),
        scratch_shapes=[
            pltpu.VMEM((CTX + 2 * HALO, CN_IN), F32),
            pltpu.VMEM((CTX, CTX_COLS), BF16),
        ],
        compiler_params=pltpu.CompilerParams(vmem_limit_bytes=VMEM_LIMIT),
        name="ctx",
    )(ctx, nw, mod3, mod3, w_xbc, wdt, dtb, alog, cw, cb)


def _scan_kernel(x_ref, b_ref, c_ref, dt_ref, h0_ref, y_ref, st):
    d = pl.program_id(0)
    c = pl.program_id(2)

    @pl.when(c == 0)
    def _():
        st[...] = h0_ref[...]

    q = CHUNK
    dtb = dt_ref[...]
    row = lax.broadcasted_iota(jnp.int32, (q, q), 0)
    col = lax.broadcasted_iota(jnp.int32, (q, q), 1)
    mask = jnp.where(d == 0, row - col, col - row) >= 0
    tri = mask.astype(BF16)
    v_hi, v_mid, v_lo = _split3(dtb)
    cum = _dot(tri, v_hi) + _dot(tri, v_mid) + _dot(tri, v_lo)
    cum_t = cum.T
    dtb_t = dtb.T
    tot = jnp.where(d == 0, cum[q - 1:q, :], cum[0:1, :])
    e_cum = jnp.exp(cum)
    e_tot = jnp.exp(tot)
    dte = jnp.exp(tot - cum) * pltpu.roll(dtb, HEADS, axis=1)
    lane = lax.broadcasted_iota(jnp.int32, (q, LANES), 1)
    left = lane < HEAD_DIM
    top = lax.broadcasted_iota(jnp.int32, (LANES, 1), 0) < HEAD_DIM

    for g in range(GROUPS):
        bg = b_ref[:, pl.ds(g * STATE, STATE)]
        cg = c_ref[:, pl.ds(g * STATE, STATE)]
        cb = _dot_nt(cg, bg)
        for pp in range(2):
            pair = 2 * g + pp
            xg = x_ref[:, pl.ds(pair * LANES, LANES)]
            res = []
            for j in range(2):
                h = 2 * pair + j
                la = HEADS + h
                seg = cum[:, la:la + 1] - cum_t[la:la + 1, :]
                dec = jnp.exp(jnp.where(mask, seg, -jnp.inf))
                m = (cb * dec * dtb_t[h:h + 1, :]).astype(BF16)
                res.append(_dot(m, xg))
            la = HEADS + 2 * pair
            y_diag = jnp.where(left, res[0], res[1])
            rows = pl.ds(pair * LANES, LANES)
            s_old = st[rows, :]
            e_col = jnp.where(left, e_cum[:, la:la + 1], e_cum[:, la + 1:la + 2])
            y_off = _dot_nt(cg, s_old.astype(BF16)) * e_col
            y_ref[:, pl.ds(pair * LANES, LANES)] = (y_diag + y_off).astype(BF16)
            w_col = jnp.where(left, dte[:, la:la + 1], dte[:, la + 1:la + 2])
            xs = (xg.astype(F32) * w_col).astype(BF16)
            cd = jnp.where(top, e_tot[:, la:la + 1], e_tot[:, la + 1:la + 2])
            st[rows, :] = s_old * cd + _dot_tn(xs, bg)


def _scan(xbc, dtda, h0):
    n = xbc.shape[0]
    bsz = n // SEQ
    nc = SEQ // CHUNK

    def tok(d, b, c):
        return b * nc + c + d * (nc - 1 - 2 * c)

    return pl.pallas_call(
        _scan_kernel,
        grid=(2, bsz, nc),
        in_specs=[
            pl.BlockSpec((CHUNK, INNER), lambda d, b, c: (tok(d, b, c), 0)),
            pl.BlockSpec((CHUNK, GN), lambda d, b, c: (tok(d, b, c), INNER // GN)),
            pl.BlockSpec((CHUNK, GN), lambda d, b, c: (tok(d, b, c), INNER // GN + 1)),
            pl.BlockSpec((None, CHUNK, LANES), lambda d, b, c: (d, tok(d, b, c), 0)),
            pl.BlockSpec((None, None, INNER, STATE), lambda d, b, c: (d, b, 0, 0)),
        ],
        out_specs=pl.BlockSpec((None, CHUNK, INNER), lambda d, b, c: (d, tok(d, b, c), 0)),
        out_shape=jax.ShapeDtypeStruct((2, n, INNER), BF16),
        scratch_shapes=[pltpu.VMEM((INNER, STATE), F32)],
        compiler_params=pltpu.CompilerParams(
            dimension_semantics=("arbitrary", "arbitrary", "arbitrary"),
            vmem_limit_bytes=VMEM_LIMIT),
        name="scan",
    )(xbc, xbc, xbc, dtda, h0)


PADR = GRID_W + 2 * HALO
CN_CONV = 128


def _conv_kernel(x_ref, nw_ref, sh_ref, sc_ref, wglu_ref, wg_ref, dw_ref, db_ref, lnw_ref, lnb_ref,
                 wco_ref, bco_ref, a_ref, pad_scr, u_scr):
    nseg = TM_CONV // GRID_W
    h = _norm_mod(x_ref[...], nw_ref[...], sh_ref[...], sc_ref[...]).astype(BF16)
    val = _dot(h, wglu_ref[:, pl.ds(0, D)])
    gate = _dot(h, wglu_ref[:, pl.ds(D, D)])
    u = val * _sigmoid(gate)
    zeros = jnp.zeros((HALO, D), F32)
    for s in range(nseg):
        pad_scr[s, pl.ds(0, HALO), :] = zeros
        pad_scr[s, pl.ds(HALO, GRID_W), :] = u[s * GRID_W:(s + 1) * GRID_W, :]
        pad_scr[s, pl.ds(HALO + GRID_W, HALO), :] = zeros

    def body(it, carry):
        s = it // (D // CN_CONV)
        cc = it % (D // CN_CONV)
        cols = pl.ds(pl.multiple_of(cc * CN_CONV, CN_CONV), CN_CONV)
        acc = jnp.broadcast_to(db_ref[:, cols], (GRID_W, CN_CONV))
        for k in range(CONV_K):
            acc = acc + pad_scr[s, pl.ds(HALO - CONV_K // 2 + k, GRID_W), cols] * dw_ref[k:k + 1, cols]
        u_scr[pl.ds(pl.multiple_of(s * GRID_W, GRID_W), GRID_W), cols] = acc
        return carry

    lax.fori_loop(0, nseg * (D // CN_CONV), body, 0)

    v = u_scr[...]
    mu = jnp.mean(v, axis=-1, keepdims=True)
    vc = v - mu
    var = jnp.mean(vc * vc, axis=-1, keepdims=True)
    ln = vc * lax.rsqrt(var + EPS) * lnw_ref[...] + lnb_ref[...]
    act = _silu(ln).astype(BF16)
    u_conv = _dot(act, wco_ref[...]) + bco_ref[...]
    g_conv = _sigmoid(_dot(h, wg_ref[...]))
    a_ref[...] = (g_conv * u_conv).astype(BF16)


def _conv_branch(x2, mod3, nw, w_glu, w_gc, dw, db, lnw, lnb, wco, bco):
    n = x2.shape[0]
    tpb = SEQ // TM_CONV
    const = lambda i: (0, 0)
    return pl.pallas_call(
        _conv_kernel,
        grid=(n // TM_CONV,),
        in_specs=[
            pl.BlockSpec((TM_CONV, D), lambda i: (i, 0)),
            pl.BlockSpec((1, D), const),
            pl.BlockSpec((None, 1, D), lambda i: (i // tpb, 0, 0)),
            pl.BlockSpec((None, 1, D), lambda i: (i // tpb, 0, 1)),
            pl.BlockSpec((D, 2 * D), const),
            pl.BlockSpec((D, D), const),
            pl.BlockSpec((CONV_K, D), const),
            pl.BlockSpec((1, D), const),
            pl.BlockSpec((1, D), const),
            pl.BlockSpec((1, D), const),
            pl.BlockSpec((D, D), const),
            pl.BlockSpec((1, D), const),
        ],
        out_specs=pl.BlockSpec((TM_CONV, D), lambda i: (i, 0)),
        out_shape=jax.ShapeDtypeStruct((n, D), BF16),
        scratch_shapes=[
            pltpu.VMEM((TM_CONV // GRID_W, PADR, D), F32),
            pltpu.VMEM((TM_CONV, D), F32),
        ],
        compiler_params=pltpu.CompilerParams(vmem_limit_bytes=VMEM_LIMIT),
        name="conv",
    )(x2, nw, mod3, mod3, w_glu, w_gc, dw, db, lnw, lnb, wco, bco)


def _merge_kernel(x_ref, yf_ref, yb_ref, xs_ref, a_ref, nw_ref, sh_ref, sc_ref, gt_ref, wz_ref, wg_ref,
                  dexp_ref, snw_ref, wso_ref, wo_ref, o_ref):
    x = x_ref[...]
    h = _norm_mod(x, nw_ref[...], sh_ref[...], sc_ref[...]).astype(BF16)
    z = _dot(h, wz_ref[...])
    y = yf_ref[...].astype(F32) + yb_ref[...].astype(F32) + dexp_ref[...] * xs_ref[...].astype(F32)
    y = y * _silu(z)
    ms = jnp.mean(y * y, axis=-1, keepdims=True)
    yn = (y * lax.rsqrt(ms + EPS) * snw_ref[...]).astype(BF16)
    u_ssd = _dot(yn, wso_ref[...])
    g_ssd = _sigmoid(_dot(h, wg_ref[...]))
    m = a_ref[...].astype(F32) + g_ssd * u_ssd
    mix = _dot(m.astype(BF16), wo_ref[...])
    o_ref[...] = x + gt_ref[...] * mix


def _merge(x2, y, xbc, a, mod3, nw, w_z, w_gs, dexp, snw, wso, wo):
    n = x2.shape[0]
    tm = TM_MERGE
    tpb = SEQ // tm
    const = lambda i: (0, 0)
    return pl.pallas_call(
        _merge_kernel,
        grid=(n // tm,),
        in_specs=[
            pl.BlockSpec((tm, D), lambda i: (i, 0)),
            pl.BlockSpec((None, tm, INNER), lambda i: (0, i, 0)),
            pl.BlockSpec((None, tm, INNER), lambda i: (1, i, 0)),
            pl.BlockSpec((tm, INNER), lambda i: (i, 0)),
            pl.BlockSpec((tm, D), lambda i: (i, 0)),
            pl.BlockSpec((1, D), const),
            pl.BlockSpec((None, 1, D), lambda i: (i // tpb, 0, 0)),
            pl.BlockSpec((None, 1, D), lambda i: (i // tpb, 0, 1)),
            pl.BlockSpec((None, 1, D), lambda i: (i // tpb, 0, 2)),
            pl.BlockSpec((D, INNER), const),
            pl.BlockSpec((D, D), const),
            pl.BlockSpec((1, INNER), const),
            pl.BlockSpec((1, INNER), const),
            pl.BlockSpec((INNER, D), const),
            pl.BlockSpec((D, D), const),
        ],
        out_specs=pl.BlockSpec((tm, D), lambda i: (i, 0)),
        out_shape=jax.ShapeDtypeStruct((n, D), F32),
        compiler_params=pltpu.CompilerParams(vmem_limit_bytes=VMEM_LIMIT),
        name="merge",
    )(x2, y, y, xbc, a, nw, mod3, mod3, mod3, w_z, w_gs, dexp, snw, wso, wo)


def _mlp_kernel(x_ref, nw_ref, sh_ref, sc_ref, gt_ref, w1_ref, w2_ref, fnw_ref, o_ref):
    x = x_ref[...]
    h = _norm_mod(x, nw_ref[...], sh_ref[...], sc_ref[...]).astype(BF16)
    acc = jnp.zeros((TM_MLP, D), F32)
    for c in range(D_FF // FF_CHUNK):
        t = jnp.maximum(_dot(h, w1_ref[:, pl.ds(c * FF_CHUNK, FF_CHUNK)]), 0.0)
        acc = acc + _dot((t * t).astype(BF16), w2_ref[pl.ds(c * FF_CHUNK, FF_CHUNK), :])
    x2 = x + gt_ref[...] * acc
    ms = jnp.mean(x2 * x2, axis=-1, keepdims=True)
    o_ref[...] = x2 * lax.rsqrt(ms + EPS) * fnw_ref[...]


def _mlp(x1, mod3, nw, w1, w2, fnw):
    n = x1.shape[0]
    tm = TM_MLP
    tpb = SEQ // tm
    const = lambda i: (0, 0)
    return pl.pallas_call(
        _mlp_kernel,
        grid=(n // tm,),
        in_specs=[
            pl.BlockSpec((tm, D), lambda i: (i, 0)),
            pl.BlockSpec((1, D), const),
            pl.BlockSpec((None, 1, D), lambda i: (i // tpb, 0, 3)),
            pl.BlockSpec((None, 1, D), lambda i: (i // tpb, 0, 4)),
            pl.BlockSpec((None, 1, D), lambda i: (i // tpb, 0, 5)),
            pl.BlockSpec((D, D_FF), const),
            pl.BlockSpec((D_FF, D), const),
            pl.BlockSpec((1, D), const),
        ],
        out_specs=pl.BlockSpec((tm, D), lambda i: (i, 0)),
        out_shape=jax.ShapeDtypeStruct((n, D), F32),
        compiler_params=pltpu.CompilerParams(vmem_limit_bytes=VMEM_LIMIT),
        name="mlp",
    )(x1, nw, mod3, mod3, mod3, w1, w2, fnw)


def kernel(x, c, ctx, c_ctx, w_ada, b_ada, norm1_w, norm2_w, w_in, conv_dw_w, conv_dw_b, conv_ln_w,
           conv_ln_b, w_conv_out, b_conv_out, ssm_conv_w, ssm_conv_b, ssm_dt_bias, ssm_a_log, ssm_d,
           ssm_norm_w, w_ssm_out, w_o, w_mlp1, w_mlp2, final_norm_w):
    bsz, seq, _ = x.shape
    assert (bsz, seq, x.shape[2]) == (8, SEQ, D) and ctx.shape[1] == CTX and w_ada.shape[0] == 1
    n = bsz * seq
    x2 = x.reshape(n, D)
    row = lambda v: v.reshape(1, -1)

    wi = w_in[0]
    w_xbc = wi[:, :COL_DT].astype(BF16)
    wdt_f = wi[:, COL_DT:COL_DT + HEADS]
    wdt_b = wi[:, COL_DT + HEADS:COL_Z]
    wdt = jnp.concatenate([wdt_f, wdt_f, wdt_b, wdt_b], axis=1)
    bf_, bb_ = ssm_dt_bias[0, 0], ssm_dt_bias[0, 1]
    dtb = row(jnp.concatenate([bf_, bf_, bb_, bb_]))
    zh = jnp.zeros((HEADS,), F32)
    alog = row(jnp.concatenate([zh, ssm_a_log[0, 0], zh, ssm_a_log[0, 1]]))
    w_z = wi[:, COL_Z:COL_GLU].astype(BF16)
    w_glu = wi[:, COL_GLU:COL_GATE].astype(BF16)
    w_gc = wi[:, COL_GATE:COL_GATE + D].astype(BF16)
    w_gs = wi[:, COL_GATE + D:].astype(BF16)
    dexp = row(jnp.repeat(ssm_d[0], HEAD_DIM))

    cond = jnp.concatenate([c, c_ctx[None, :], jnp.zeros((16 - bsz - 1, D), F32)], axis=0)
    mod = _ada(cond, w_ada[0], row(b_ada[0]))
    mod3 = mod.reshape(16, 1, 6 * D)

    nw1 = row(norm1_w[0])
    cw, cb = ssm_conv_w[0], row(ssm_conv_b[0])
    h0 = _ctx(ctx, mod3, nw1, w_xbc, wdt, dtb, alog, cw, cb)
    xbc, dtda = _ssd_in(x2, mod3, nw1, w_xbc, wdt, dtb, alog, cw, cb)
    y = _scan(xbc, dtda, h0)
    a = _conv_branch(x2, mod3, nw1, w_glu, w_gc, conv_dw_w[0], row(conv_dw_b[0]), row(conv_ln_w[0]),
                     row(conv_ln_b[0]), w_conv_out[0].astype(BF16), row(b_conv_out[0]))
    x1 = _merge(x2, y, xbc, a, mod3, nw1, w_z, w_gs, dexp, row(ssm_norm_w[0]),
                w_ssm_out[0].astype(BF16), w_o[0].astype(BF16))
    out = _mlp(x1, mod3, row(norm2_w[0]), w_mlp1[0].astype(BF16), w_mlp2[0].astype(BF16),
               row(final_norm_w))
    return out.reshape(bsz, seq, D)
```

```python
import functools

import jax
import jax.numpy as jnp
from jax import lax
from jax.experimental import pallas as pl
from jax.experimental.pallas import tpu as pltpu

F32 = jnp.float32
BF16 = jnp.bfloat16

D = 1024
SEQ = 4096
CTX = 256
GRID_W = 64
CONV_K = 31
INNER = 2048
HEADS = 32
HEAD_DIM = 64
GROUPS = 8
STATE = 128
GN = GROUPS * STATE
SSM_K = 5
CHUNK = 128
XBC = INNER + 2 * GN
D_FF = 4 * D
COL_DT = XBC
COL_Z = COL_DT + 2 * HEADS
COL_GLU = COL_Z + INNER
COL_GATE = COL_GLU + 2 * D
EPS = 1e-6

LANES = 128
HALO = 16
VMEM_LIMIT = 56 * 1024 * 1024

TM_IN = 512
CN_IN = 512
TM_CONV = 512
TM_MERGE = 256
TM_MLP = 512
FF_CHUNK = 1024


def _dot(a, b):
    return jnp.dot(a, b, preferred_element_type=F32)


def _dot_nt(a, b):
    return lax.dot_general(a, b, (((1,), (1,)), ((), ())), preferred_element_type=F32)


def _dot_tn(a, b):
    return lax.dot_general(a, b, (((0,), (0,)), ((), ())), preferred_element_type=F32)


def _sigmoid(v):
    return 1.0 / (1.0 + jnp.exp(-v))


def _silu(v):
    return v * _sigmoid(v)


def _norm_mod(x, nw, shift, scale):
    ms = jnp.mean(x * x, axis=-1, keepdims=True)
    y = x * lax.rsqrt(ms + EPS) * nw
    return y * (1.0 + scale) + shift


def _split3(v):
    hi = v.astype(BF16)
    r1 = v - hi.astype(F32)
    mid = r1.astype(BF16)
    lo = (r1 - mid.astype(F32)).astype(BF16)
    return hi, mid, lo


def _dt_lanes(h, wdt_ref, dtb_ref, alog_ref):
    w = wdt_ref[...]
    w_hi = w.astype(BF16)
    w_lo = (w - w_hi.astype(F32)).astype(BF16)
    h_hi = h.astype(BF16)
    h_lo = (h - h_hi.astype(F32)).astype(BF16)
    raw = _dot(h_hi, w_hi) + _dot(h_hi, w_lo) + _dot(h_lo, w_hi)
    v = raw + dtb_ref[...]
    dt = jnp.maximum(v, 0.0) + jnp.log1p(jnp.exp(-jnp.abs(v)))
    lane = lax.broadcasted_iota(jnp.int32, (1, LANES), 1)
    mult = jnp.where((lane % 64) < HEADS, 1.0, -jnp.exp(alog_ref[...]))
    return dt * mult


def _ada_kernel(cond_ref, w_ref, b_ref, o_ref):
    c = cond_ref[...]
    s = _silu(c)
    o_ref[...] = _dot(s.astype(BF16), w_ref[...].astype(BF16)) + b_ref[...]


def _ada(cond, w, b):
    n = w.shape[1]
    tn = 1024
    return pl.pallas_call(
        _ada_kernel,
        grid=(n // tn,),
        in_specs=[
            pl.BlockSpec((16, D), lambda j: (0, 0)),
            pl.BlockSpec((D, tn), lambda j: (0, j)),
            pl.BlockSpec((1, tn), lambda j: (0, j)),
        ],
        out_specs=pl.BlockSpec((16, tn), lambda j: (0, j)),
        out_shape=jax.ShapeDtypeStruct((16, n), F32),
        name="ada",
    )(cond, w, b)


def _conv5_silu(p_scr, rows, step, cw, cb):
    acc = cb
    for k in range(SSM_K):
        acc = acc + p_scr[pl.ds(HALO - (SSM_K // 2) * step + k * step, rows), :] * cw[k:k + 1, :]
    return _silu(acc)


SUB = 8
NSEG_IN = SUB
SEG_IN = TM_IN // NSEG_IN


def _to_seg_major(dst, src, nseg, seg):
    for j in range(dst.shape[0]):
        for s in range(nseg):
            dst[j, pl.ds(s, seg, stride=nseg), :] = src[pl.ds(s * seg, seg), pl.ds(j * LANES, LANES)]


def _from_seg_major(dst, col0, src, nseg, seg):
    for j in range(src.shape[0]):
        for s in range(nseg):
            dst[pl.ds(s * seg, seg), pl.ds(col0 + j * LANES, LANES)] = (
                src[j, pl.ds(s, seg, stride=nseg), :].astype(dst.dtype))


def _load_cols(ref):
    return jnp.concatenate([ref[j] for j in range(ref.shape[0])], axis=1)


def _store_cols(ref, v):
    for j in range(ref.shape[0]):
        ref[j] = v[:, j * LANES:(j + 1) * LANES]


def _ssd_in_kernel(xp_ref, x_ref, xn_ref, nw_ref, sh_ref, sc_ref, w_ref, wdt_ref, dtb_ref,
                   alog_ref, cw_ref, cb_ref, xbc_ref, dt_ref, xs_scr, h_scr, p_scr, r_scr, v_scr):
    i = pl.program_id(0)
    tpb = SEQ // TM_IN
    first = (i % tpb) == 0
    last = (i % tpb) == tpb - 1
    nw, sh, sc = nw_ref[...], sh_ref[...], sc_ref[...]
    _to_seg_major(xs_scr, x_ref, NSEG_IN, SEG_IN)
    hm = _norm_mod(_load_cols(xs_scr), nw, sh, sc)
    h_scr[pl.ds(0, TM_IN), :] = hm.astype(BF16)
    h_scr[pl.ds(TM_IN, HALO), :] = _norm_mod(xp_ref[...], nw, sh, sc).astype(BF16)
    h_scr[pl.ds(TM_IN + HALO, HALO), :] = _norm_mod(xn_ref[...], nw, sh, sc).astype(BF16)

    val = _dt_lanes(hm, wdt_ref, dtb_ref, alog_ref)
    lane = lax.broadcasted_iota(jnp.int32, (1, LANES), 1)
    v_scr[0] = jnp.where(lane < 64, val, 0.0)
    v_scr[1] = jnp.where(lane < 64, pltpu.roll(val, 64, axis=1), 0.0)
    for d in range(2):
        for s in range(NSEG_IN):
            dt_ref[d, pl.ds(s * SEG_IN, SEG_IN), :] = v_scr[d, pl.ds(s, SEG_IN, stride=NSEG_IN), :]

    sub = lax.broadcasted_iota(jnp.int32, (SUB, CN_IN), 0)
    keep_prev = jnp.where(first, 0.0, 1.0)
    keep_next = jnp.where(last, 0.0, 1.0)
    nblk = TM_IN // SUB
    for c in range(XBC // CN_IN):
        cols = pl.ds(c * CN_IN, CN_IN)
        p = _dot(h_scr[...], w_ref[:, cols])
        pp = p_scr.at[c % 2]
        rr = r_scr.at[c % 2]
        pp[pl.ds(HALO, TM_IN), :] = p[0:TM_IN]
        prev = p[TM_IN + HALO - SUB:TM_IN + HALO] * keep_prev
        nxt = p[TM_IN + HALO:TM_IN + HALO + SUB] * keep_next
        blk = lambda b: p[b * SUB:(b + 1) * SUB]
        pp[pl.ds(HALO - SUB, SUB), :] = jnp.where(
            sub == 0, pltpu.roll(prev, 1, axis=0), pltpu.roll(blk(nblk - 1), 1, axis=0))
        pp[pl.ds(HALO - 2 * SUB, SUB), :] = jnp.where(
            sub == 0, pltpu.roll(prev, 2, axis=0), pltpu.roll(blk(nblk - 2), 1, axis=0))
        pp[pl.ds(HALO + TM_IN, SUB), :] = jnp.where(
            sub == SUB - 1, pltpu.roll(nxt, SUB - 1, axis=0), pltpu.roll(blk(0), SUB - 1, axis=0))
        pp[pl.ds(HALO + TM_IN + SUB, SUB), :] = jnp.where(
            sub == SUB - 1, pltpu.roll(nxt, SUB - 2, axis=0), pltpu.roll(blk(1), SUB - 1, axis=0))
        _store_cols(rr, _conv5_silu(pp, TM_IN, SUB, cw_ref[:, cols], cb_ref[:, cols]))
        _from_seg_major(xbc_ref, c * CN_IN, rr, NSEG_IN, SEG_IN)


def _ssd_in(x2, mod3, nw, w_xbc, wdt, dtb, alog, cw, cb):
    n = x2.shape[0]
    nt = n // TM_IN
    tpb = SEQ // TM_IN
    hb = TM_IN // HALO
    nhb = n // HALO
    const = lambda i: (0, 0)
    return pl.pallas_call(
        _ssd_in_kernel,
        grid=(nt,),
        in_specs=[
            pl.BlockSpec((HALO, D), lambda i: (jnp.maximum(i * hb - 1, 0), 0)),
            pl.BlockSpec((TM_IN, D), lambda i: (i, 0)),
            pl.BlockSpec((HALO, D), lambda i: (jnp.minimum((i + 1) * hb, nhb - 1), 0)),
            pl.BlockSpec((1, D), const),
            pl.BlockSpec((None, 1, D), lambda i: (i // tpb, 0, 0)),
            pl.BlockSpec((None, 1, D), lambda i: (i // tpb, 0, 1)),
            pl.BlockSpec((D, XBC), const),
            pl.BlockSpec((D, LANES), const),
            pl.BlockSpec((1, LANES), const),
            pl.BlockSpec((1, LANES), const),
            pl.BlockSpec((SSM_K, XBC), const),
            pl.BlockSpec((1, XBC), const),
        ],
        out_specs=[
            pl.BlockSpec((TM_IN, XBC), lambda i: (i, 0)),
            pl.BlockSpec((2, TM_IN, LANES), lambda i: (0, i, 0)),
        ],
        out_shape=[
            jax.ShapeDtypeStruct((n, XBC), BF16),
            jax.ShapeDtypeStruct((2, n, LANES), F32),
        ],
        scratch_shapes=[
            pltpu.VMEM((D // LANES, TM_IN, LANES), F32),
            pltpu.VMEM((TM_IN + 2 * HALO, D), BF16),
            pltpu.VMEM((2, TM_IN + 2 * HALO, CN_IN), F32),
            pltpu.VMEM((2, CN_IN // LANES, TM_IN, LANES), F32),
            pltpu.VMEM((2, TM_IN, LANES), F32),
        ],
        compiler_params=pltpu.CompilerParams(vmem_limit_bytes=VMEM_LIMIT),
        name="ssd_in",
    )(x2, x2, x2, nw, mod3, mod3, w_xbc, wdt, dtb, alog, cw, cb)


CTX_COLS = INNER + GN


def _ctx_kernel(c_ref, nw_ref, sh_ref, sc_ref, w_ref, wdt_ref, dtb_ref, alog_ref, cw_ref, cb_ref,
                h0_ref, p_scr, xb_scr):
    hc = _norm_mod(c_ref[...], nw_ref[...], sh_ref[...], sc_ref[...])
    hb = hc.astype(BF16)
    p_scr[pl.ds(0, HALO), :] = jnp.zeros((HALO, CN_IN), F32)
    p_scr[pl.ds(HALO + CTX, HALO), :] = jnp.zeros((HALO, CN_IN), F32)
    for c in range(CTX_COLS // CN_IN):
        cols = pl.ds(c * CN_IN, CN_IN)
        p_scr[pl.ds(HALO, CTX), :] = _dot(hb, w_ref[:, cols])
        xb_scr[:, cols] = _conv5_silu(p_scr, CTX, 1, cw_ref[:, cols], cb_ref[:, cols]).astype(BF16)

    val = _dt_lanes(hc, wdt_ref, dtb_ref, alog_ref)
    row = lax.broadcasted_iota(jnp.int32, (CTX, CTX), 0)
    col = lax.broadcasted_iota(jnp.int32, (CTX, CTX), 1)
    tri = (row >= col).astype(BF16)
    v_hi, v_mid, v_lo = _split3(val)
    cum = _dot(tri, v_hi) + _dot(tri, v_mid) + _dot(tri, v_lo)
    tot = cum[CTX - 1:CTX, :]
    dts = pltpu.roll(val, HEADS, axis=1)
    lane = lax.broadcasted_iota(jnp.int32, (1, LANES), 1)
    wgt = jnp.where(lane < 64, jnp.exp(tot - cum), jnp.exp(cum - val)) * dts
    lane2 = lax.broadcasted_iota(jnp.int32, (CTX, LANES), 1)
    for d in range(2):
        base = d * 64 + HEADS
        for pair in range(HEADS // 2):
            g = pair // 2
            la = base + 2 * pair
            wcol = jnp.where(lane2 < HEAD_DIM, wgt[:, la:la + 1], wgt[:, la + 1:la + 2])
            xs = (xb_scr[:, pl.ds(pair * LANES, LANES)].astype(F32) * wcol).astype(BF16)
            bg = xb_scr[:, pl.ds(INNER + g * STATE, STATE)]
            h0_ref[d, pl.ds(pair * LANES, LANES), :] = _dot_tn(xs, bg)


def _ctx(ctx, mod3, nw, w_xbc, wdt, dtb, alog, cw, cb):
    b = ctx.shape[0]
    const = lambda i: (0, 0)
    return pl.pallas_call(
        _ctx_kernel,
        grid=(b,),
        in_specs=[
            pl.BlockSpec((None, CTX, D), lambda i: (i, 0, 0)),
            pl.BlockSpec((1, D), const),
            pl.BlockSpec((None, 1, D), lambda i: (8, 0, 0)),
            pl.BlockSpec((None, 1, D), lambda i: (8, 0, 1)),
            pl.BlockSpec((D, CTX_COLS), const),
            pl.BlockSpec((D, LANES), const),
            pl.BlockSpec((1, LANES), const),
            pl.BlockSpec((1, LANES), const),
            pl.BlockSpec((SSM_K, CTX_COLS), const),
            pl.BlockSpec((1, CTX_COLS), const),
        ],
        out_specs=pl.BlockSpec((2, None, INNER, STATE), lambda i: (0, i, 0, 0)),
        out_shape=jax.ShapeDtypeStruct((2, b, INNER, STATE), F32),
        scratch_shapes=[
            pltpu.VMEM((CTX + 2 * HALO, CN_IN), F32),
            pltpu.VMEM((CTX, CTX_COLS), BF16),
        ],
        compiler_params=pltpu.CompilerParams(vmem_limit_bytes=VMEM_LIMIT),
        name="ctx",
    )(ctx, nw, mod3, mod3, w_xbc, wdt, dtb, alog, cw, cb)


def _scan_kernel(x_ref, b_ref, c_ref, dt_ref, h0_ref, y_ref, st):
    d = pl.program_id(0)
    c = pl.program_id(2)

    @pl.when(c == 0)
    def _():
        st[...] = h0_ref[...]

    q = CHUNK
    dtb = dt_ref[...]
    row = lax.broadcasted_iota(jnp.int32, (q, q), 0)
    col = lax.broadcasted_iota(jnp.int32, (q, q), 1)
    mask = jnp.where(d == 0, row - col, col - row) >= 0
    tri = mask.astype(BF16)
    v_hi, v_mid, v_lo = _split3(dtb)
    cum = _dot(tri, v_hi) + _dot(tri, v_mid) + _dot(tri, v_lo)
    cum_t = cum.T
    dtb_t = dtb.T
    tot = jnp.where(d == 0, cum[q - 1:q, :], cum[0:1, :])
    e_cum = jnp.exp(cum)
    e_tot = jnp.exp(tot)
    dte = jnp.exp(tot - cum) * pltpu.roll(dtb, HEADS, axis=1)
    lane = lax.broadcasted_iota(jnp.int32, (q, LANES), 1)
    left = lane < HEAD_DIM
    top = lax.broadcasted_iota(jnp.int32, (LANES, 1), 0) < HEAD_DIM

    for g in range(GROUPS):
        bg = b_ref[:, pl.ds(g * STATE, STATE)]
        cg = c_ref[:, pl.ds(g * STATE, STATE)]
        cb = _dot_nt(cg, bg)
        for pp in range(2):
            pair = 2 * g + pp
            xg = x_ref[:, pl.ds(pair * LANES, LANES)]
            res = []
            for j in range(2):
                h = 2 * pair + j
                la = HEADS + h
                seg = cum[:, la:la + 1] - cum_t[la:la + 1, :]
                dec = jnp.exp(jnp.where(mask, seg, -jnp.inf))
                m = (cb * dec * dtb_t[h:h + 1, :]).astype(BF16)
                res.append(_dot(m, xg))
            la = HEADS + 2 * pair
            y_diag = jnp.where(left, res[0], res[1])
            rows = pl.ds(pair * LANES, LANES)
            s_old = st[rows, :]
            e_col = jnp.where(left, e_cum[:, la:la + 1], e_cum[:, la + 1:la + 2])
            y_off = _dot_nt(cg, s_old.astype(BF16)) * e_col
            y_ref[:, pl.ds(pair * LANES, LANES)] = (y_diag + y_off).astype(BF16)
            w_col = jnp.where(left, dte[:, la:la + 1], dte[:, la + 1:la + 2])
            xs = (xg.astype(F32) * w_col).astype(BF16)
            cd = jnp.where(top, e_tot[:, la:la + 1], e_tot[:, la + 1:la + 2])
            st[rows, :] = s_old * cd + _dot_tn(xs, bg)


def _scan(xbc, dtda, h0):
    n = xbc.shape[0]
    bsz = n // SEQ
    nc = SEQ // CHUNK

    def tok(d, b, c):
        return b * nc + c + d * (nc - 1 - 2 * c)

    return pl.pallas_call(
        _scan_kernel,
        grid=(2, bsz, nc),
        in_specs=[
            pl.BlockSpec((CHUNK, INNER), lambda d, b, c: (tok(d, b, c), 0)),
            pl.BlockSpec((CHUNK, GN), lambda d, b, c: (tok(d, b, c), INNER // GN)),
            pl.BlockSpec((CHUNK, GN), lambda d, b, c: (tok(d, b, c), INNER // GN + 1)),
            pl.BlockSpec((None, CHUNK, LANES), lambda d, b, c: (d, tok(d, b, c), 0)),
            pl.BlockSpec((None, None, INNER, STATE), lambda d, b, c: (d, b, 0, 0)),
        ],
        out_specs=pl.BlockSpec((None, CHUNK, INNER), lambda d, b, c: (d, tok(d, b, c), 0)),
        out_shape=jax.ShapeDtypeStruct((2, n, INNER), BF16),
        scratch_shapes=[pltpu.VMEM((INNER, STATE), F32)],
        compiler_params=pltpu.CompilerParams(
            dimension_semantics=("arbitrary", "arbitrary", "arbitrary"),
            vmem_limit_bytes=VMEM_LIMIT),
        name="scan",
    )(xbc, xbc, xbc, dtda, h0)


NSEG_CONV = TM_CONV // GRID_W
assert NSEG_CONV == SUB
PAD_CONV = (CONV_K // 2) * NSEG_CONV
RB_CONV = 64
MM_CONV = 256


def _conv_kernel(x_ref, nw_ref, sh_ref, sc_ref, wglu_ref, wg_ref, dw_ref, db_ref, lnw_ref, lnb_ref,
                 wco_ref, bco_ref, a_ref, xs_scr, pad_scr, u_scr):
    _to_seg_major(xs_scr, x_ref, NSEG_CONV, GRID_W)
    h = _norm_mod(_load_cols(xs_scr), nw_ref[...], sh_ref[...], sc_ref[...]).astype(BF16)
    zeros = jnp.zeros((PAD_CONV, LANES), F32)
    per_mm = MM_CONV // LANES
    for cb in range(D // MM_CONV):
        val = _dot(h, wglu_ref[:, pl.ds(cb * MM_CONV, MM_CONV)])
        gate = _dot(h, wglu_ref[:, pl.ds(D + cb * MM_CONV, MM_CONV)])
        u = val * _sigmoid(gate)
        for j in range(per_mm):
            cc = cb * per_mm + j
            pad_scr[cc, pl.ds(0, PAD_CONV), :] = zeros
            pad_scr[cc, pl.ds(PAD_CONV, TM_CONV), :] = u[:, j * LANES:(j + 1) * LANES]
            pad_scr[cc, pl.ds(PAD_CONV + TM_CONV, PAD_CONV), :] = zeros
            for rb in range(TM_CONV // RB_CONV):
                r0 = rb * RB_CONV
                acc = jnp.broadcast_to(db_ref[cc], (RB_CONV, LANES))
                for k in range(CONV_K):
                    acc = acc + pad_scr[cc, pl.ds(r0 + k * NSEG_CONV, RB_CONV), :] * dw_ref[cc, pl.ds(k, 1), :]
                u_scr[cc, pl.ds(r0, RB_CONV), :] = acc

    v = _load_cols(u_scr)
    mu = jnp.mean(v, axis=-1, keepdims=True)
    vc = v - mu
    var = jnp.mean(vc * vc, axis=-1, keepdims=True)
    ln = vc * lax.rsqrt(var + EPS) * lnw_ref[...] + lnb_ref[...]
    act = _silu(ln).astype(BF16)
    u_conv = _dot(act, wco_ref[...]) + bco_ref[...]
    g_conv = _sigmoid(_dot(h, wg_ref[...]))
    _store_cols(u_scr, g_conv * u_conv)
    _from_seg_major(a_ref, 0, u_scr, NSEG_CONV, GRID_W)


def _conv_branch(x2, mod3, nw, w_glu, w_gc, dw, db, lnw, lnb, wco, bco):
    n = x2.shape[0]
    tpb = SEQ // TM_CONV
    const = lambda i: (0, 0)
    return pl.pallas_call(
        _conv_kernel,
        grid=(n // TM_CONV,),
        in_specs=[
            pl.BlockSpec((TM_CONV, D), lambda i: (i, 0)),
            pl.BlockSpec((1, D), const),
            pl.BlockSpec((None, 1, D), lambda i: (i // tpb, 0, 0)),
            pl.BlockSpec((None, 1, D), lambda i: (i // tpb, 0, 1)),
            pl.BlockSpec((D, 2 * D), const),
            pl.BlockSpec((D, D), const),
            pl.BlockSpec((D // LANES, CONV_K, LANES), lambda i: (0, 0, 0)),
            pl.BlockSpec((D // LANES, 1, LANES), lambda i: (0, 0, 0)),
            pl.BlockSpec((1, D), const),
            pl.BlockSpec((1, D), const),
            pl.BlockSpec((D, D), const),
            pl.BlockSpec((1, D), const),
        ],
        out_specs=pl.BlockSpec((TM_CONV, D), lambda i: (i, 0)),
        out_shape=jax.ShapeDtypeStruct((n, D), BF16),
        scratch_shapes=[
            pltpu.VMEM((D // LANES, TM_CONV, LANES), F32),
            pltpu.VMEM((D // LANES, TM_CONV + 2 * PAD_CONV, LANES), F32),
            pltpu.VMEM((D // LANES, TM_CONV, LANES), F32),
        ],
        compiler_params=pltpu.CompilerParams(vmem_limit_bytes=VMEM_LIMIT),
        name="conv",
    )(x2, nw, mod3, mod3, w_glu, w_gc, dw, db, lnw, lnb, wco, bco)


def _merge_kernel(x_ref, yf_ref, yb_ref, xs_ref, a_ref, nw_ref, sh_ref, sc_ref, gt_ref, wz_ref, wg_ref,
                  dexp_ref, snw_ref, wso_ref, wo_ref, o_ref):
    x = x_ref[...]
    h = _norm_mod(x, nw_ref[...], sh_ref[...], sc_ref[...]).astype(BF16)
    z = _dot(h, wz_ref[...])
    y = yf_ref[...].astype(F32) + yb_ref[...].astype(F32) + dexp_ref[...] * xs_ref[...].astype(F32)
    y = y * _silu(z)
    ms = jnp.mean(y * y, axis=-1, keepdims=True)
    yn = (y * lax.rsqrt(ms + EPS) * snw_ref[...]).astype(BF16)
    u_ssd = _dot(yn, wso_ref[...])
    g_ssd = _sigmoid(_dot(h, wg_ref[...]))
    m = a_ref[...].astype(F32) + g_ssd * u_ssd
    mix = _dot(m.astype(BF16), wo_ref[...])
    o_ref[...] = x + gt_ref[...] * mix


def _merge(x2, y, xbc, a, mod3, nw, w_z, w_gs, dexp, snw, wso, wo):
    n = x2.shape[0]
    tm = TM_MERGE
    tpb = SEQ // tm
    const = lambda i: (0, 0)
    return pl.pallas_call(
        _merge_kernel,
        grid=(n // tm,),
        in_specs=[
            pl.BlockSpec((tm, D), lambda i: (i, 0)),
            pl.BlockSpec((None, tm, INNER), lambda i: (0, i, 0)),
            pl.BlockSpec((None, tm, INNER), lambda i: (1, i, 0)),
            pl.BlockSpec((tm, INNER), lambda i: (i, 0)),
            pl.BlockSpec((tm, D), lambda i: (i, 0)),
            pl.BlockSpec((1, D), const),
            pl.BlockSpec((None, 1, D), lambda i: (i // tpb, 0, 0)),
            pl.BlockSpec((None, 1, D), lambda i: (i // tpb, 0, 1)),
            pl.BlockSpec((None, 1, D), lambda i: (i // tpb, 0, 2)),
            pl.BlockSpec((D, INNER), const),
            pl.BlockSpec((D, D), const),
            pl.BlockSpec((1, INNER), const),
            pl.BlockSpec((1, INNER), const),
            pl.BlockSpec((INNER, D), const),
            pl.BlockSpec((D, D), const),
        ],
        out_specs=pl.BlockSpec((tm, D), lambda i: (i, 0)),
        out_shape=jax.ShapeDtypeStruct((n, D), F32),
        compiler_params=pltpu.CompilerParams(vmem_limit_bytes=VMEM_LIMIT),
        name="merge",
    )(x2, y, y, xbc, a, nw, mod3, mod3, mod3, w_z, w_gs, dexp, snw, wso, wo)


def _mlp_kernel(x_ref, nw_ref, sh_ref, sc_ref, gt_ref, w1_ref, w2_ref, fnw_ref, o_ref):
    x = x_ref[...]
    h = _norm_mod(x, nw_ref[...], sh_ref[...], sc_ref[...]).astype(BF16)
    acc = jnp.zeros((TM_MLP, D), F32)
    for c in range(D_FF // FF_CHUNK):
        t = jnp.maximum(_dot(h, w1_ref[:, pl.ds(c * FF_CHUNK, FF_CHUNK)]), 0.0)
        acc = acc + _dot((t * t).astype(BF16), w2_ref[pl.ds(c * FF_CHUNK, FF_CHUNK), :])
    x2 = x + gt_ref[...] * acc
    ms = jnp.mean(x2 * x2, axis=-1, keepdims=True)
    o_ref[...] = x2 * lax.rsqrt(ms + EPS) * fnw_ref[...]


def _mlp(x1, mod3, nw, w1, w2, fnw):
    n = x1.shape[0]
    tm = TM_MLP
    tpb = SEQ // tm
    const = lambda i: (0, 0)
    return pl.pallas_call(
        _mlp_kernel,
        grid=(n // tm,),
        in_specs=[
            pl.BlockSpec((tm, D), lambda i: (i, 0)),
            pl.BlockSpec((1, D), const),
            pl.BlockSpec((None, 1, D), lambda i: (i // tpb, 0, 3)),
            pl.BlockSpec((None, 1, D), lambda i: (i // tpb, 0, 4)),
            pl.BlockSpec((None, 1, D), lambda i: (i // tpb, 0, 5)),
            pl.BlockSpec((D, D_FF), const),
            pl.BlockSpec((D_FF, D), const),
            pl.BlockSpec((1, D), const),
        ],
        out_specs=pl.BlockSpec((tm, D), lambda i: (i, 0)),
        out_shape=jax.ShapeDtypeStruct((n, D), F32),
        compiler_params=pltpu.CompilerParams(vmem_limit_bytes=VMEM_LIMIT),
        name="mlp",
    )(x1, nw, mod3, mod3, mod3, w1, w2, fnw)


def kernel(x, c, ctx, c_ctx, w_ada, b_ada, norm1_w, norm2_w, w_in, conv_dw_w, conv_dw_b, conv_ln_w,
           conv_ln_b, w_conv_out, b_conv_out, ssm_conv_w, ssm_conv_b, ssm_dt_bias, ssm_a_log, ssm_d,
           ssm_norm_w, w_ssm_out, w_o, w_mlp1, w_mlp2, final_norm_w):
    bsz, seq, _ = x.shape
    assert (bsz, seq, x.shape[2]) == (8, SEQ, D) and ctx.shape[1] == CTX and w_ada.shape[0] == 1
    n = bsz * seq
    x2 = x.reshape(n, D)
    row = lambda v: v.reshape(1, -1)

    wi = w_in[0]
    w_xbc = wi[:, :COL_DT].astype(BF16)
    wdt_f = wi[:, COL_DT:COL_DT + HEADS]
    wdt_b = wi[:, COL_DT + HEADS:COL_Z]
    wdt = jnp.concatenate([wdt_f, wdt_f, wdt_b, wdt_b], axis=1)
    bf_, bb_ = ssm_dt_bias[0, 0], ssm_dt_bias[0, 1]
    dtb = row(jnp.concatenate([bf_, bf_, bb_, bb_]))
    zh = jnp.zeros((HEADS,), F32)
    alog = row(jnp.concatenate([zh, ssm_a_log[0, 0], zh, ssm_a_log[0, 1]]))
    w_z = wi[:, COL_Z:COL_GLU].astype(BF16)
    w_glu = wi[:, COL_GLU:COL_GATE].astype(BF16)
    w_gc = wi[:, COL_GATE:COL_GATE + D].astype(BF16)
    w_gs = wi[:, COL_GATE + D:].astype(BF16)
    dexp = row(jnp.repeat(ssm_d[0], HEAD_DIM))

    cond = jnp.concatenate([c, c_ctx[None, :], jnp.zeros((16 - bsz - 1, D), F32)], axis=0)
    mod = _ada(cond, w_ada[0], row(b_ada[0]))
    mod3 = mod.reshape(16, 1, 6 * D)

    nw1 = row(norm1_w[0])
    cw, cb = ssm_conv_w[0], row(ssm_conv_b[0])
    h0 = _ctx(ctx, mod3, nw1, w_xbc, wdt, dtb, alog, cw, cb)
    xbc, dtda = _ssd_in(x2, mod3, nw1, w_xbc, wdt, dtb, alog, cw, cb)
    y = _scan(xbc, dtda, h0)
    dw3 = conv_dw_w[0].reshape(CONV_K, D // LANES, LANES).transpose(1, 0, 2)
    db3 = conv_dw_b[0].reshape(D // LANES, 1, LANES)
    a = _conv_branch(x2, mod3, nw1, w_glu, w_gc, dw3, db3, row(conv_ln_w[0]),
                     row(conv_ln_b[0]), w_conv_out[0].astype(BF16), row(b_conv_out[0]))
    x1 = _merge(x2, y, xbc, a, mod3, nw1, w_z, w_gs, dexp, row(ssm_norm_w[0]),
                w_ssm_out[0].astype(BF16), w_o[0].astype(BF16))
    out = _mlp(x1, mod3, row(norm2_w[0]), w_mlp1[0].astype(BF16), w_mlp2[0].astype(BF16),
               row(final_norm_w))
    return out.reshape(bsz, seq, D)
```

```python
import functools

import jax
import jax.numpy as jnp
from jax import lax
from jax.experimental import pallas as pl
from jax.experimental.pallas import tpu as pltpu

F32 = jnp.float32
BF16 = jnp.bfloat16

D = 1024
SEQ = 4096
CTX = 256
GRID_W = 64
CONV_K = 31
INNER = 2048
HEADS = 32
HEAD_DIM = 64
GROUPS = 8
STATE = 128
GN = GROUPS * STATE
SSM_K = 5
CHUNK = 128
XBC = INNER + 2 * GN
D_FF = 4 * D
COL_DT = XBC
COL_Z = COL_DT + 2 * HEADS
COL_GLU = COL_Z + INNER
COL_GATE = COL_GLU + 2 * D
EPS = 1e-6

LANES = 128
HALO = 16
VMEM_LIMIT = 56 * 1024 * 1024

QC = 256
TM_IN = 512
CN_IN = 512
TM_CONV = 512
TM_MERGE = 256
TM_MLP = 512
FF_CHUNK = 1024


def _dot(a, b):
    return jnp.dot(a, b, preferred_element_type=F32)


def _dot_nt(a, b):
    return lax.dot_general(a, b, (((1,), (1,)), ((), ())), preferred_element_type=F32)


def _dot_tn(a, b):
    return lax.dot_general(a, b, (((0,), (0,)), ((), ())), preferred_element_type=F32)


def _sigmoid(v):
    return 1.0 / (1.0 + jnp.exp(-v))


def _silu(v):
    return v * _sigmoid(v)


def _norm_mod(x, nw, shift, scale):
    ms = jnp.mean(x * x, axis=-1, keepdims=True)
    y = x * lax.rsqrt(ms + EPS) * nw
    return y * (1.0 + scale) + shift


def _split3(v):
    hi = v.astype(BF16)
    r1 = v - hi.astype(F32)
    mid = r1.astype(BF16)
    lo = (r1 - mid.astype(F32)).astype(BF16)
    return hi, mid, lo


def _dt_lanes(h, wdt_ref, dtb_ref, alog_ref):
    w = wdt_ref[...]
    w_hi = w.astype(BF16)
    w_lo = (w - w_hi.astype(F32)).astype(BF16)
    h_hi = h.astype(BF16)
    h_lo = (h - h_hi.astype(F32)).astype(BF16)
    raw = _dot(h_hi, w_hi) + _dot(h_hi, w_lo) + _dot(h_lo, w_hi)
    v = raw + dtb_ref[...]
    dt = jnp.maximum(v, 0.0) + jnp.log1p(jnp.exp(-jnp.abs(v)))
    lane = lax.broadcasted_iota(jnp.int32, (1, LANES), 1)
    mult = jnp.where((lane % 64) < HEADS, 1.0, -jnp.exp(alog_ref[...]))
    return dt * mult


def _ada_kernel(cond_ref, w_ref, b_ref, o_ref):
    c = cond_ref[...]
    s = _silu(c)
    o_ref[...] = _dot(s.astype(BF16), w_ref[...].astype(BF16)) + b_ref[...]


def _ada(cond, w, b):
    n = w.shape[1]
    tn = 1024
    return pl.pallas_call(
        _ada_kernel,
        grid=(n // tn,),
        in_specs=[
            pl.BlockSpec((16, D), lambda j: (0, 0)),
            pl.BlockSpec((D, tn), lambda j: (0, j)),
            pl.BlockSpec((1, tn), lambda j: (0, j)),
        ],
        out_specs=pl.BlockSpec((16, tn), lambda j: (0, j)),
        out_shape=jax.ShapeDtypeStruct((16, n), F32),
        name="ada",
    )(cond, w, b)


def _conv5_silu(p_scr, rows, step, cw, cb):
    acc = cb
    for k in range(SSM_K):
        acc = acc + p_scr[pl.ds(HALO - (SSM_K // 2) * step + k * step, rows), :] * cw[k:k + 1, :]
    return _silu(acc)


SUB = 8
NSEG_IN = SUB
SEG_IN = TM_IN // NSEG_IN


def _to_seg_major(dst, src, nseg, seg):
    for j in range(dst.shape[0]):
        for s in range(nseg):
            dst[j, pl.ds(s, seg, stride=nseg), :] = src[pl.ds(s * seg, seg), pl.ds(j * LANES, LANES)]


def _from_seg_major(dst, col0, src, nseg, seg):
    for j in range(src.shape[0]):
        for s in range(nseg):
            dst[pl.ds(s * seg, seg), pl.ds(col0 + j * LANES, LANES)] = (
                src[j, pl.ds(s, seg, stride=nseg), :].astype(dst.dtype))


def _load_cols(ref):
    return jnp.concatenate([ref[j] for j in range(ref.shape[0])], axis=1)


def _store_cols(ref, v):
    for j in range(ref.shape[0]):
        ref[j] = v[:, j * LANES:(j + 1) * LANES]


def _ssd_in_kernel(xp_ref, x_ref, xn_ref, nw_ref, sh_ref, sc_ref, w_ref, wdt_ref, dtb_ref,
                   alog_ref, cw_ref, cb_ref, xt_ref, bc_ref, dt_ref, xs_scr, h_scr, p_scr, r_scr, v_scr):
    i = pl.program_id(0)
    tpb = SEQ // TM_IN
    first = (i % tpb) == 0
    last = (i % tpb) == tpb - 1
    nw, sh, sc = nw_ref[...], sh_ref[...], sc_ref[...]
    _to_seg_major(xs_scr, x_ref, NSEG_IN, SEG_IN)
    hm = _norm_mod(_load_cols(xs_scr), nw, sh, sc)
    h_scr[pl.ds(0, TM_IN), :] = hm.astype(BF16)
    h_scr[pl.ds(TM_IN, HALO), :] = _norm_mod(xp_ref[...], nw, sh, sc).astype(BF16)
    h_scr[pl.ds(TM_IN + HALO, HALO), :] = _norm_mod(xn_ref[...], nw, sh, sc).astype(BF16)

    val = _dt_lanes(hm, wdt_ref, dtb_ref, alog_ref)
    lane = lax.broadcasted_iota(jnp.int32, (1, LANES), 1)
    v_scr[0] = jnp.where(lane < 64, val, 0.0)
    v_scr[1] = jnp.where(lane < 64, pltpu.roll(val, 64, axis=1), 0.0)
    for d in range(2):
        for s in range(NSEG_IN):
            dt_ref[d, pl.ds(s * SEG_IN, SEG_IN), :] = v_scr[d, pl.ds(s, SEG_IN, stride=NSEG_IN), :]

    sub = lax.broadcasted_iota(jnp.int32, (SUB, CN_IN), 0)
    keep_prev = jnp.where(first, 0.0, 1.0)
    keep_next = jnp.where(last, 0.0, 1.0)
    nblk = TM_IN // SUB
    for c in range(XBC // CN_IN):
        cols = pl.ds(c * CN_IN, CN_IN)
        p = _dot(h_scr[...], w_ref[:, cols])
        pp = p_scr.at[c % 2]
        rr = r_scr.at[c % 2]
        pp[pl.ds(HALO, TM_IN), :] = p[0:TM_IN]
        prev = p[TM_IN + HALO - SUB:TM_IN + HALO] * keep_prev
        nxt = p[TM_IN + HALO:TM_IN + HALO + SUB] * keep_next
        blk = lambda b: p[b * SUB:(b + 1) * SUB]
        pp[pl.ds(HALO - SUB, SUB), :] = jnp.where(
            sub == 0, pltpu.roll(prev, 1, axis=0), pltpu.roll(blk(nblk - 1), 1, axis=0))
        pp[pl.ds(HALO - 2 * SUB, SUB), :] = jnp.where(
            sub == 0, pltpu.roll(prev, 2, axis=0), pltpu.roll(blk(nblk - 2), 1, axis=0))
        pp[pl.ds(HALO + TM_IN, SUB), :] = jnp.where(
            sub == SUB - 1, pltpu.roll(nxt, SUB - 1, axis=0), pltpu.roll(blk(0), SUB - 1, axis=0))
        pp[pl.ds(HALO + TM_IN + SUB, SUB), :] = jnp.where(
            sub == SUB - 1, pltpu.roll(nxt, SUB - 2, axis=0), pltpu.roll(blk(1), SUB - 1, axis=0))
        _store_cols(rr, _conv5_silu(pp, TM_IN, SUB, cw_ref[:, cols], cb_ref[:, cols]))
        if c * CN_IN >= INNER:
            _from_seg_major(bc_ref, c * CN_IN - INNER, rr, NSEG_IN, SEG_IN)
        else:
            for j in range(CN_IN // LANES):
                for m in range(TM_IN // LANES):
                    tok = jnp.concatenate(
                        [rr[j, pl.ds(2 * m + e, SEG_IN, stride=NSEG_IN), :] for e in range(LANES // SEG_IN)],
                        axis=0)
                    xt_ref[(m * LANES) // QC, pl.ds(c * CN_IN + j * LANES, LANES),
                           pl.ds((m * LANES) % QC, LANES)] = tok.T.astype(BF16)


def _ssd_in(x2, mod3, nw, w_xbc, wdt, dtb, alog, cw, cb):
    n = x2.shape[0]
    nt = n // TM_IN
    tpb = SEQ // TM_IN
    hb = TM_IN // HALO
    nhb = n // HALO
    const = lambda i: (0, 0)
    return pl.pallas_call(
        _ssd_in_kernel,
        grid=(nt,),
        in_specs=[
            pl.BlockSpec((HALO, D), lambda i: (jnp.maximum(i * hb - 1, 0), 0)),
            pl.BlockSpec((TM_IN, D), lambda i: (i, 0)),
            pl.BlockSpec((HALO, D), lambda i: (jnp.minimum((i + 1) * hb, nhb - 1), 0)),
            pl.BlockSpec((1, D), const),
            pl.BlockSpec((None, 1, D), lambda i: (i // tpb, 0, 0)),
            pl.BlockSpec((None, 1, D), lambda i: (i // tpb, 0, 1)),
            pl.BlockSpec((D, XBC), const),
            pl.BlockSpec((D, LANES), const),
            pl.BlockSpec((1, LANES), const),
            pl.BlockSpec((1, LANES), const),
            pl.BlockSpec((SSM_K, XBC), const),
            pl.BlockSpec((1, XBC), const),
        ],
        out_specs=[
            pl.BlockSpec((TM_IN // QC, INNER, QC), lambda i: (i, 0, 0)),
            pl.BlockSpec((TM_IN, 2 * GN), lambda i: (i, 0)),
            pl.BlockSpec((2, TM_IN, LANES), lambda i: (0, i, 0)),
        ],
        out_shape=[
            jax.ShapeDtypeStruct((n // QC, INNER, QC), BF16),
            jax.ShapeDtypeStruct((n, 2 * GN), BF16),
            jax.ShapeDtypeStruct((2, n, LANES), F32),
        ],
        scratch_shapes=[
            pltpu.VMEM((D // LANES, TM_IN, LANES), F32),
            pltpu.VMEM((TM_IN + 2 * HALO, D), BF16),
            pltpu.VMEM((2, TM_IN + 2 * HALO, CN_IN), F32),
            pltpu.VMEM((2, CN_IN // LANES, TM_IN, LANES), F32),
            pltpu.VMEM((2, TM_IN, LANES), F32),
        ],
        compiler_params=pltpu.CompilerParams(vmem_limit_bytes=VMEM_LIMIT),
        name="ssd_in",
    )(x2, x2, x2, nw, mod3, mod3, w_xbc, wdt, dtb, alog, cw, cb)


CTX_COLS = INNER + GN


def _ctx_kernel(c_ref, nw_ref, sh_ref, sc_ref, w_ref, wdt_ref, dtb_ref, alog_ref, cw_ref, cb_ref,
                h0_ref, p_scr, xb_scr):
    hc = _norm_mod(c_ref[...], nw_ref[...], sh_ref[...], sc_ref[...])
    hb = hc.astype(BF16)
    p_scr[pl.ds(0, HALO), :] = jnp.zeros((HALO, CN_IN), F32)
    p_scr[pl.ds(HALO + CTX, HALO), :] = jnp.zeros((HALO, CN_IN), F32)
    for c in range(CTX_COLS // CN_IN):
        cols = pl.ds(c * CN_IN, CN_IN)
        p_scr[pl.ds(HALO, CTX), :] = _dot(hb, w_ref[:, cols])
        xb_scr[:, cols] = _conv5_silu(p_scr, CTX, 1, cw_ref[:, cols], cb_ref[:, cols]).astype(BF16)

    val = _dt_lanes(hc, wdt_ref, dtb_ref, alog_ref)
    row = lax.broadcasted_iota(jnp.int32, (CTX, CTX), 0)
    col = lax.broadcasted_iota(jnp.int32, (CTX, CTX), 1)
    tri = (row >= col).astype(BF16)
    v_hi, v_mid, v_lo = _split3(val)
    cum = _dot(tri, v_hi) + _dot(tri, v_mid) + _dot(tri, v_lo)
    tot = cum[CTX - 1:CTX, :]
    dts = pltpu.roll(val, HEADS, axis=1)
    lane = lax.broadcasted_iota(jnp.int32, (1, LANES), 1)
    wgt = jnp.where(lane < 64, jnp.exp(tot - cum), jnp.exp(cum - val)) * dts
    lane2 = lax.broadcasted_iota(jnp.int32, (CTX, LANES), 1)
    for d in range(2):
        base = d * 64 + HEADS
        for pair in range(HEADS // 2):
            g = pair // 2
            la = base + 2 * pair
            wcol = jnp.where(lane2 < HEAD_DIM, wgt[:, la:la + 1], wgt[:, la + 1:la + 2])
            xs = (xb_scr[:, pl.ds(pair * LANES, LANES)].astype(F32) * wcol).astype(BF16)
            bg = xb_scr[:, pl.ds(INNER + g * STATE, STATE)]
            h0_ref[d, pl.ds(pair * LANES, LANES), :] = _dot_tn(xs, bg)


def _ctx(ctx, mod3, nw, w_xbc, wdt, dtb, alog, cw, cb):
    b = ctx.shape[0]
    const = lambda i: (0, 0)
    return pl.pallas_call(
        _ctx_kernel,
        grid=(b,),
        in_specs=[
            pl.BlockSpec((None, CTX, D), lambda i: (i, 0, 0)),
            pl.BlockSpec((1, D), const),
            pl.BlockSpec((None, 1, D), lambda i: (8, 0, 0)),
            pl.BlockSpec((None, 1, D), lambda i: (8, 0, 1)),
            pl.BlockSpec((D, CTX_COLS), const),
            pl.BlockSpec((D, LANES), const),
            pl.BlockSpec((1, LANES), const),
            pl.BlockSpec((1, LANES), const),
            pl.BlockSpec((SSM_K, CTX_COLS), const),
            pl.BlockSpec((1, CTX_COLS), const),
        ],
        out_specs=pl.BlockSpec((2, None, INNER, STATE), lambda i: (0, i, 0, 0)),
        out_shape=jax.ShapeDtypeStruct((2, b, INNER, STATE), F32),
        scratch_shapes=[
            pltpu.VMEM((CTX + 2 * HALO, CN_IN), F32),
            pltpu.VMEM((CTX, CTX_COLS), BF16),
        ],
        compiler_params=pltpu.CompilerParams(vmem_limit_bytes=VMEM_LIMIT),
        name="ctx",
    )(ctx, nw, mod3, mod3, w_xbc, wdt, dtb, alog, cw, cb)


LOG2E = 1.4426950408889634


def _scan_kernel(d_smem, xt_ref, b_ref, c_ref, dt_ref, h0_ref, y_ref, st, *, bwd):
    c = pl.program_id(1)

    @pl.when(c == 0)
    def _():
        st[...] = h0_ref[...]

    q = QC
    hq = q // 2
    dtb = dt_ref[...]
    row = lax.broadcasted_iota(jnp.int32, (q, q), 0)
    col = lax.broadcasted_iota(jnp.int32, (q, q), 1)
    tri = ((col >= row) if bwd else (row >= col)).astype(BF16)
    v_hi, v_mid, v_lo = _split3(dtb)
    cum = (_dot(tri, v_hi) + _dot(tri, v_mid) + _dot(tri, v_lo)) * LOG2E
    cum_t = cum.T[HEADS:2 * HEADS, :]
    dt_t = dtb.T[0:HEADS, :]
    tot_t = cum_t[:, 0:1] if bwd else cum_t[:, q - 1:q]
    ecum_t = jnp.exp2(cum_t)
    dte_t = jnp.exp2(tot_t - cum_t) * dt_t
    etot = jnp.exp2(tot_t)
    keep = ((row >= col) if bwd else (col >= row)).astype(F32)
    fr = pl.ds(hq, hq) if bwd else pl.ds(0, hq)
    pr = pl.ds(0, hq) if bwd else pl.ds(hq, hq)
    fs = slice(hq, q) if bwd else slice(0, hq)
    ps = slice(0, hq) if bwd else slice(hq, q)
    zero = jnp.zeros((hq, hq), BF16)

    gp = (HEADS // GROUPS) * HEAD_DIM
    for g in range(GROUPS):
        bg = b_ref[:, pl.ds(g * STATE, STATE)]
        cg = c_ref[:, pl.ds(g * STATE, STATE)]
        cbf = _dot_nt(b_ref[fr, pl.ds(g * STATE, STATE)], cg) * keep[fs, :]
        cbp = _dot_nt(b_ref[pr, pl.ds(g * STATE, STATE)], c_ref[pr, pl.ds(g * STATE, STATE)]) * keep[ps, ps]
        s_old = st[pl.ds(g * gp, gp), :]
        y_off = _dot_nt(s_old.astype(BF16), cg)
        xs, cds = [], []
        for j in range(HEADS // GROUPS):
            h = g * (HEADS // GROUPS) + j
            r0 = g * gp + j * HEAD_DIM
            crow = cum_t[h:h + 1, :]
            ccol = cum[:, HEADS + h:HEADS + h + 1]
            mf = (cbf * jnp.exp2(jnp.minimum(crow - ccol[fs], 0.0))).astype(BF16)
            mp = (cbp * jnp.exp2(jnp.minimum(crow[:, ps] - ccol[ps], 0.0))).astype(BF16)
            if bwd:
                mt = jnp.concatenate([jnp.concatenate([mp, zero], axis=1), mf], axis=0)
            else:
                mt = jnp.concatenate([mf, jnp.concatenate([zero, mp], axis=1)], axis=0)
            xh = xt_ref[pl.ds(r0, HEAD_DIM), :].astype(F32)
            y_diag = _dot((xh * dt_t[h:h + 1, :]).astype(BF16), mt)
            yh = y_diag + y_off[j * HEAD_DIM:(j + 1) * HEAD_DIM] * ecum_t[h:h + 1, :]
            if not bwd:
                yh = yh + d_smem[h] * xh
            y_ref[pl.ds(r0, HEAD_DIM), :] = yh.astype(BF16)
            xs.append((xh * dte_t[h:h + 1, :]).astype(BF16))
            cds.append(jnp.broadcast_to(etot[h:h + 1, :], (HEAD_DIM, STATE)))
        s_new = _dot(jnp.concatenate(xs, axis=0), bg)
        st[pl.ds(g * gp, gp), :] = s_old * jnp.concatenate(cds, axis=0) + s_new


def _scan(bwd, ssm_d, xt, bc, dtda, h0):
    n = bc.shape[0]
    bsz = n // SEQ
    nc = SEQ // QC
    d = int(bwd)

    def chunk(b, c):
        return b * nc + (nc - 1 - c if bwd else c)

    return pl.pallas_call(
        functools.partial(_scan_kernel, bwd=bwd),
        grid=(bsz, nc),
        in_specs=[
            pl.BlockSpec(memory_space=pltpu.SMEM),
            pl.BlockSpec((None, INNER, QC), lambda b, c: (chunk(b, c), 0, 0)),
            pl.BlockSpec((QC, GN), lambda b, c: (chunk(b, c), 0)),
            pl.BlockSpec((QC, GN), lambda b, c: (chunk(b, c), 1)),
            pl.BlockSpec((None, QC, LANES), lambda b, c: (d, chunk(b, c), 0)),
            pl.BlockSpec((None, None, INNER, STATE), lambda b, c: (d, b, 0, 0)),
        ],
        out_specs=pl.BlockSpec((None, INNER, QC), lambda b, c: (chunk(b, c), 0, 0)),
        out_shape=jax.ShapeDtypeStruct((n // QC, INNER, QC), BF16),
        scratch_shapes=[pltpu.VMEM((INNER, STATE), F32)],
        compiler_params=pltpu.CompilerParams(
            dimension_semantics=("arbitrary", "arbitrary"),
            vmem_limit_bytes=VMEM_LIMIT),
        name="scan_bwd" if bwd else "scan_fwd",
    )(ssm_d, xt, bc, bc, dtda, h0)


NSEG_CONV = TM_CONV // GRID_W
assert NSEG_CONV == SUB
PAD_CONV = (CONV_K // 2) * NSEG_CONV
RB_CONV = 64
MM_CONV = 256


def _conv_kernel(x_ref, nw_ref, sh_ref, sc_ref, wglu_ref, wg_ref, dw_ref, db_ref, lnw_ref, lnb_ref,
                 wco_ref, bco_ref, a_ref, xs_scr, pad_scr, u_scr):
    _to_seg_major(xs_scr, x_ref, NSEG_CONV, GRID_W)
    h = _norm_mod(_load_cols(xs_scr), nw_ref[...], sh_ref[...], sc_ref[...]).astype(BF16)
    zeros = jnp.zeros((PAD_CONV, LANES), F32)
    per_mm = MM_CONV // LANES
    for cb in range(D // MM_CONV):
        val = _dot(h, wglu_ref[:, pl.ds(cb * MM_CONV, MM_CONV)])
        gate = _dot(h, wglu_ref[:, pl.ds(D + cb * MM_CONV, MM_CONV)])
        u = val * _sigmoid(gate)
        for j in range(per_mm):
            cc = cb * per_mm + j
            pad_scr[cc, pl.ds(0, PAD_CONV), :] = zeros
            pad_scr[cc, pl.ds(PAD_CONV, TM_CONV), :] = u[:, j * LANES:(j + 1) * LANES]
            pad_scr[cc, pl.ds(PAD_CONV + TM_CONV, PAD_CONV), :] = zeros
            for rb in range(TM_CONV // RB_CONV):
                r0 = rb * RB_CONV
                acc = jnp.broadcast_to(db_ref[cc], (RB_CONV, LANES))
                for k in range(CONV_K):
                    acc = acc + pad_scr[cc, pl.ds(r0 + k * NSEG_CONV, RB_CONV), :] * dw_ref[cc, pl.ds(k, 1), :]
                u_scr[cc, pl.ds(r0, RB_CONV), :] = acc

    v = _load_cols(u_scr)
    mu = jnp.mean(v, axis=-1, keepdims=True)
    vc = v - mu
    var = jnp.mean(vc * vc, axis=-1, keepdims=True)
    ln = vc * lax.rsqrt(var + EPS) * lnw_ref[...] + lnb_ref[...]
    act = _silu(ln).astype(BF16)
    u_conv = _dot(act, wco_ref[...]) + bco_ref[...]
    g_conv = _sigmoid(_dot(h, wg_ref[...]))
    _store_cols(u_scr, g_conv * u_conv)
    _from_seg_major(a_ref, 0, u_scr, NSEG_CONV, GRID_W)


def _conv_branch(x2, mod3, nw, w_glu, w_gc, dw, db, lnw, lnb, wco, bco):
    n = x2.shape[0]
    tpb = SEQ // TM_CONV
    const = lambda i: (0, 0)
    return pl.pallas_call(
        _conv_kernel,
        grid=(n // TM_CONV,),
        in_specs=[
            pl.BlockSpec((TM_CONV, D), lambda i: (i, 0)),
            pl.BlockSpec((1, D), const),
            pl.BlockSpec((None, 1, D), lambda i: (i // tpb, 0, 0)),
            pl.BlockSpec((None, 1, D), lambda i: (i // tpb, 0, 1)),
            pl.BlockSpec((D, 2 * D), const),
            pl.BlockSpec((D, D), const),
            pl.BlockSpec((D // LANES, CONV_K, LANES), lambda i: (0, 0, 0)),
            pl.BlockSpec((D // LANES, 1, LANES), lambda i: (0, 0, 0)),
            pl.BlockSpec((1, D), const),
            pl.BlockSpec((1, D), const),
            pl.BlockSpec((D, D), const),
            pl.BlockSpec((1, D), const),
        ],
        out_specs=pl.BlockSpec((TM_CONV, D), lambda i: (i, 0)),
        out_shape=jax.ShapeDtypeStruct((n, D), BF16),
        scratch_shapes=[
            pltpu.VMEM((D // LANES, TM_CONV, LANES), F32),
            pltpu.VMEM((D // LANES, TM_CONV + 2 * PAD_CONV, LANES), F32),
            pltpu.VMEM((D // LANES, TM_CONV, LANES), F32),
        ],
        compiler_params=pltpu.CompilerParams(vmem_limit_bytes=VMEM_LIMIT),
        name="conv",
    )(x2, nw, mod3, mod3, w_glu, w_gc, dw, db, lnw, lnb, wco, bco)


def _merge_kernel(x_ref, yf_ref, yb_ref, a_ref, nw_ref, sh_ref, sc_ref, gt_ref, wz_ref, wg_ref,
                  snw_ref, wso_ref, wo_ref, o_ref):
    x = x_ref[...]
    h = _norm_mod(x, nw_ref[...], sh_ref[...], sc_ref[...]).astype(BF16)
    z = _dot(h, wz_ref[...])
    y = (yf_ref[...].astype(F32) + yb_ref[...].astype(F32)).T
    y = y * _silu(z)
    ms = jnp.mean(y * y, axis=-1, keepdims=True)
    yn = (y * lax.rsqrt(ms + EPS) * snw_ref[...]).astype(BF16)
    u_ssd = _dot(yn, wso_ref[...])
    g_ssd = _sigmoid(_dot(h, wg_ref[...]))
    m = a_ref[...].astype(F32) + g_ssd * u_ssd
    mix = _dot(m.astype(BF16), wo_ref[...])
    o_ref[...] = x + gt_ref[...] * mix


def _merge(x2, yf, yb, a, mod3, nw, w_z, w_gs, snw, wso, wo):
    n = x2.shape[0]
    tm = TM_MERGE
    assert tm == QC
    tpb = SEQ // tm
    const = lambda i: (0, 0)
    return pl.pallas_call(
        _merge_kernel,
        grid=(n // tm,),
        in_specs=[
            pl.BlockSpec((tm, D), lambda i: (i, 0)),
            pl.BlockSpec((None, INNER, QC), lambda i: (i, 0, 0)),
            pl.BlockSpec((None, INNER, QC), lambda i: (i, 0, 0)),
            pl.BlockSpec((tm, D), lambda i: (i, 0)),
            pl.BlockSpec((1, D), const),
            pl.BlockSpec((None, 1, D), lambda i: (i // tpb, 0, 0)),
            pl.BlockSpec((None, 1, D), lambda i: (i // tpb, 0, 1)),
            pl.BlockSpec((None, 1, D), lambda i: (i // tpb, 0, 2)),
            pl.BlockSpec((D, INNER), const),
            pl.BlockSpec((D, D), const),
            pl.BlockSpec((1, INNER), const),
            pl.BlockSpec((INNER, D), const),
            pl.BlockSpec((D, D), const),
        ],
        out_specs=pl.BlockSpec((tm, D), lambda i: (i, 0)),
        out_shape=jax.ShapeDtypeStruct((n, D), F32),
        compiler_params=pltpu.CompilerParams(vmem_limit_bytes=VMEM_LIMIT),
        name="merge",
    )(x2, yf, yb, a, nw, mod3, mod3, mod3, w_z, w_gs, snw, wso, wo)


def _mlp_kernel(x_ref, nw_ref, sh_ref, sc_ref, gt_ref, w1_ref, w2_ref, fnw_ref, o_ref):
    x = x_ref[...]
    h = _norm_mod(x, nw_ref[...], sh_ref[...], sc_ref[...]).astype(BF16)
    acc = jnp.zeros((TM_MLP, D), F32)
    for c in range(D_FF // FF_CHUNK):
        t = jnp.maximum(_dot(h, w1_ref[:, pl.ds(c * FF_CHUNK, FF_CHUNK)]), 0.0)
        acc = acc + _dot((t * t).astype(BF16), w2_ref[pl.ds(c * FF_CHUNK, FF_CHUNK), :])
    x2 = x + gt_ref[...] * acc
    ms = jnp.mean(x2 * x2, axis=-1, keepdims=True)
    o_ref[...] = x2 * lax.rsqrt(ms + EPS) * fnw_ref[...]


def _mlp(x1, mod3, nw, w1, w2, fnw):
    n = x1.shape[0]
    tm = TM_MLP
    tpb = SEQ // tm
    const = lambda i: (0, 0)
    return pl.pallas_call(
        _mlp_kernel,
        grid=(n // tm,),
        in_specs=[
            pl.BlockSpec((tm, D), lambda i: (i, 0)),
            pl.BlockSpec((1, D), const),
            pl.BlockSpec((None, 1, D), lambda i: (i // tpb, 0, 3)),
            pl.BlockSpec((None, 1, D), lambda i: (i // tpb, 0, 4)),
            pl.BlockSpec((None, 1, D), lambda i: (i // tpb, 0, 5)),
            pl.BlockSpec((D, D_FF), const),
            pl.BlockSpec((D_FF, D), const),
            pl.BlockSpec((1, D), const),
        ],
        out_specs=pl.BlockSpec((tm, D), lambda i: (i, 0)),
        out_shape=jax.ShapeDtypeStruct((n, D), F32),
        compiler_params=pltpu.CompilerParams(vmem_limit_bytes=VMEM_LIMIT),
        name="mlp",
    )(x1, nw, mod3, mod3, mod3, w1, w2, fnw)


def kernel(x, c, ctx, c_ctx, w_ada, b_ada, norm1_w, norm2_w, w_in, conv_dw_w, conv_dw_b, conv_ln_w,
           conv_ln_b, w_conv_out, b_conv_out, ssm_conv_w, ssm_conv_b, ssm_dt_bias, ssm_a_log, ssm_d,
           ssm_norm_w, w_ssm_out, w_o, w_mlp1, w_mlp2, final_norm_w):
    bsz, seq, _ = x.shape
    assert (bsz, seq, x.shape[2]) == (8, SEQ, D) and ctx.shape[1] == CTX and w_ada.shape[0] == 1
    n = bsz * seq
    x2 = x.reshape(n, D)
    row = lambda v: v.reshape(1, -1)

    wi = w_in[0]
    w_xbc = wi[:, :COL_DT].astype(BF16)
    wdt_f = wi[:, COL_DT:COL_DT + HEADS]
    wdt_b = wi[:, COL_DT + HEADS:COL_Z]
    wdt = jnp.concatenate([wdt_f, wdt_f, wdt_b, wdt_b], axis=1)
    bf_, bb_ = ssm_dt_bias[0, 0], ssm_dt_bias[0, 1]
    dtb = row(jnp.concatenate([bf_, bf_, bb_, bb_]))
    zh = jnp.zeros((HEADS,), F32)
    alog = row(jnp.concatenate([zh, ssm_a_log[0, 0], zh, ssm_a_log[0, 1]]))
    w_z = wi[:, COL_Z:COL_GLU].astype(BF16)
    w_glu = wi[:, COL_GLU:COL_GATE].astype(BF16)
    w_gc = wi[:, COL_GATE:COL_GATE + D].astype(BF16)
    w_gs = wi[:, COL_GATE + D:].astype(BF16)

    cond = jnp.concatenate([c, c_ctx[None, :], jnp.zeros((16 - bsz - 1, D), F32)], axis=0)
    mod = _ada(cond, w_ada[0], row(b_ada[0]))
    mod3 = mod.reshape(16, 1, 6 * D)

    nw1 = row(norm1_w[0])
    cw, cb = ssm_conv_w[0], row(ssm_conv_b[0])
    h0 = _ctx(ctx, mod3, nw1, w_xbc, wdt, dtb, alog, cw, cb)
    xt, bc, dtda = _ssd_in(x2, mod3, nw1, w_xbc, wdt, dtb, alog, cw, cb)
    yf = _scan(False, ssm_d[0], xt, bc, dtda, h0)
    yb = _scan(True, ssm_d[0], xt, bc, dtda, h0)
    dw3 = conv_dw_w[0].reshape(CONV_K, D // LANES, LANES).transpose(1, 0, 2)
    db3 = conv_dw_b[0].reshape(D // LANES, 1, LANES)
    a = _conv_branch(x2, mod3, nw1, w_glu, w_gc, dw3, db3, row(conv_ln_w[0]),
                     row(conv_ln_b[0]), w_conv_out[0].astype(BF16), row(b_conv_out[0]))
    x1 = _merge(x2, yf, yb, a, mod3, nw1, w_z, w_gs, row(ssm_norm_w[0]),
                w_ssm_out[0].astype(BF16), w_o[0].astype(BF16))
    out = _mlp(x1, mod3, row(norm2_w[0]), w_mlp1[0].astype(BF16), w_mlp2[0].astype(BF16),
               row(final_norm_w))
    return out.reshape(bsz, seq, D)
```

```python
import functools

import jax
import jax.numpy as jnp
from jax import lax
from jax.experimental import pallas as pl
from jax.experimental.pallas import tpu as pltpu

F32 = jnp.float32
BF16 = jnp.bfloat16

D = 1024
SEQ = 4096
CTX = 256
GRID_W = 64
CONV_K = 31
INNER = 2048
HEADS = 32
HEAD_DIM = 64
GROUPS = 8
STATE = 128
GN = GROUPS * STATE
SSM_K = 5
CHUNK = 128
XBC = INNER + 2 * GN
D_FF = 4 * D
COL_DT = XBC
COL_Z = COL_DT + 2 * HEADS
COL_GLU = COL_Z + INNER
COL_GATE = COL_GLU + 2 * D
EPS = 1e-6

LANES = 128
HALO = 16
VMEM_LIMIT = 56 * 1024 * 1024

QC = 256
TM_IN = 512
CN_IN = 512
TM_CONV = 512
TM_MERGE = 256
TM_MLP = 512
FF_CHUNK = 1024


def _dot(a, b):
    return jnp.dot(a, b, preferred_element_type=F32)


def _dot_nt(a, b):
    return lax.dot_general(a, b, (((1,), (1,)), ((), ())), preferred_element_type=F32)


def _dot_tn(a, b):
    return lax.dot_general(a, b, (((0,), (0,)), ((), ())), preferred_element_type=F32)


LOG2E = 1.4426950408889634


def _sigmoid(v):
    return 1.0 / (1.0 + jnp.exp2(v * (-LOG2E)))


def _silu(v):
    return v * _sigmoid(v)


def _norm_mod(x, nw, shift, scale):
    ms = jnp.mean(x * x, axis=-1, keepdims=True)
    return x * lax.rsqrt(ms + EPS) * (nw * (1.0 + scale)) + shift


def _split3(v):
    hi = v.astype(BF16)
    r1 = v - hi.astype(F32)
    mid = r1.astype(BF16)
    lo = (r1 - mid.astype(F32)).astype(BF16)
    return hi, mid, lo


def _dt_lanes(h, wdt_ref, dtb_ref, alog_ref):
    w = wdt_ref[...]
    w_hi = w.astype(BF16)
    w_lo = (w - w_hi.astype(F32)).astype(BF16)
    h_hi = h.astype(BF16)
    h_lo = (h - h_hi.astype(F32)).astype(BF16)
    both = _dot(h_hi, jnp.concatenate([w_hi, w_lo], axis=1))
    raw = both[:, :LANES] + both[:, LANES:] + _dot(h_lo, w_hi)
    v = raw + dtb_ref[...]
    dt = jnp.maximum(v, 0.0) + jnp.log1p(jnp.exp(-jnp.abs(v)))
    lane = lax.broadcasted_iota(jnp.int32, (1, LANES), 1)
    mult = jnp.where((lane % 64) < HEADS, 1.0, -jnp.exp(alog_ref[...]))
    return dt * mult


def _ada_kernel(cond_ref, w_ref, b_ref, o_ref):
    c = cond_ref[...]
    s = _silu(c)
    o_ref[...] = _dot(s.astype(BF16), w_ref[...].astype(BF16)) + b_ref[...]


def _ada(cond, w, b):
    n = w.shape[1]
    tn = 1024
    return pl.pallas_call(
        _ada_kernel,
        grid=(n // tn,),
        in_specs=[
            pl.BlockSpec((16, D), lambda j: (0, 0)),
            pl.BlockSpec((D, tn), lambda j: (0, j)),
            pl.BlockSpec((1, tn), lambda j: (0, j)),
        ],
        out_specs=pl.BlockSpec((16, tn), lambda j: (0, j)),
        out_shape=jax.ShapeDtypeStruct((16, n), F32),
        name="ada",
    )(cond, w, b)


def _conv5_silu(p_scr, rows, step, cw, cb):
    acc = cb
    for k in range(SSM_K):
        acc = acc + p_scr[pl.ds(HALO - (SSM_K // 2) * step + k * step, rows), :] * cw[k:k + 1, :]
    return _silu(acc)


SUB = 8
NSEG_IN = SUB
SEG_IN = TM_IN // NSEG_IN


def _to_seg_major(dst, src, nseg, seg):
    for j in range(dst.shape[0]):
        for s in range(nseg):
            dst[j, pl.ds(s, seg, stride=nseg), :] = src[pl.ds(s * seg, seg), pl.ds(j * LANES, LANES)]


def _from_seg_major(dst, col0, src, nseg, seg):
    for j in range(src.shape[0]):
        for s in range(nseg):
            dst[pl.ds(s * seg, seg), pl.ds(col0 + j * LANES, LANES)] = (
                src[j, pl.ds(s, seg, stride=nseg), :].astype(dst.dtype))


def _load_cols(ref):
    return jnp.concatenate([ref[j] for j in range(ref.shape[0])], axis=1)


def _store_cols(ref, v):
    for j in range(ref.shape[0]):
        ref[j] = v[:, j * LANES:(j + 1) * LANES]


def _ssd_in_kernel(xp_ref, x_ref, xn_ref, nw_ref, sh_ref, sc_ref, w_ref, wdt_ref, dtb_ref,
                   alog_ref, cw_ref, cb_ref, xt_ref, bc_ref, dt_ref, xs_scr, h_scr, p_scr, r_scr, v_scr):
    i = pl.program_id(0)
    tpb = SEQ // TM_IN
    first = (i % tpb) == 0
    last = (i % tpb) == tpb - 1
    nw, sh, sc = nw_ref[...], sh_ref[...], sc_ref[...]
    _to_seg_major(xs_scr, x_ref, NSEG_IN, SEG_IN)
    hm = _norm_mod(_load_cols(xs_scr), nw, sh, sc)
    h_scr[pl.ds(0, TM_IN), :] = hm.astype(BF16)
    h_scr[pl.ds(TM_IN, HALO), :] = _norm_mod(xp_ref[...], nw, sh, sc).astype(BF16)
    h_scr[pl.ds(TM_IN + HALO, HALO), :] = _norm_mod(xn_ref[...], nw, sh, sc).astype(BF16)

    val = _dt_lanes(hm, wdt_ref, dtb_ref, alog_ref)
    lane = lax.broadcasted_iota(jnp.int32, (1, LANES), 1)
    v_scr[0] = jnp.where(lane < 64, val, 0.0)
    v_scr[1] = jnp.where(lane < 64, pltpu.roll(val, 64, axis=1), 0.0)
    for d in range(2):
        for s in range(NSEG_IN):
            dt_ref[d, pl.ds(s * SEG_IN, SEG_IN), :] = v_scr[d, pl.ds(s, SEG_IN, stride=NSEG_IN), :]

    sub = lax.broadcasted_iota(jnp.int32, (SUB, CN_IN), 0)
    keep_prev = jnp.where(first, 0.0, 1.0)
    keep_next = jnp.where(last, 0.0, 1.0)
    nblk = TM_IN // SUB
    nchunk = XBC // CN_IN
    project = lambda c: _dot(h_scr[...], w_ref[:, pl.ds(c * CN_IN, CN_IN)])
    p_next = project(0)
    for c in range(nchunk):
        cols = pl.ds(c * CN_IN, CN_IN)
        p = p_next
        if c + 1 < nchunk:
            p_next = project(c + 1)
        pp = p_scr.at[c % 2]
        rr = r_scr.at[c % 2]
        pp[pl.ds(HALO, TM_IN), :] = p[0:TM_IN]
        prev = p[TM_IN + HALO - SUB:TM_IN + HALO] * keep_prev
        nxt = p[TM_IN + HALO:TM_IN + HALO + SUB] * keep_next
        blk = lambda b: p[b * SUB:(b + 1) * SUB]
        pp[pl.ds(HALO - SUB, SUB), :] = jnp.where(
            sub == 0, pltpu.roll(prev, 1, axis=0), pltpu.roll(blk(nblk - 1), 1, axis=0))
        pp[pl.ds(HALO - 2 * SUB, SUB), :] = jnp.where(
            sub == 0, pltpu.roll(prev, 2, axis=0), pltpu.roll(blk(nblk - 2), 1, axis=0))
        pp[pl.ds(HALO + TM_IN, SUB), :] = jnp.where(
            sub == SUB - 1, pltpu.roll(nxt, SUB - 1, axis=0), pltpu.roll(blk(0), SUB - 1, axis=0))
        pp[pl.ds(HALO + TM_IN + SUB, SUB), :] = jnp.where(
            sub == SUB - 1, pltpu.roll(nxt, SUB - 2, axis=0), pltpu.roll(blk(1), SUB - 1, axis=0))
        _store_cols(rr, _conv5_silu(pp, TM_IN, SUB, cw_ref[:, cols], cb_ref[:, cols]))
        if c * CN_IN >= INNER:
            _from_seg_major(bc_ref, c * CN_IN - INNER, rr, NSEG_IN, SEG_IN)
        else:
            for j in range(CN_IN // LANES):
                for m in range(TM_IN // LANES):
                    tok = jnp.concatenate(
                        [rr[j, pl.ds(2 * m + e, SEG_IN, stride=NSEG_IN), :] for e in range(LANES // SEG_IN)],
                        axis=0)
                    xt_ref[(m * LANES) // QC, pl.ds(c * CN_IN + j * LANES, LANES),
                           pl.ds((m * LANES) % QC, LANES)] = tok.T.astype(BF16)


def _ssd_in(x2, mod3, nw, w_xbc, wdt, dtb, alog, cw, cb):
    n = x2.shape[0]
    nt = n // TM_IN
    tpb = SEQ // TM_IN
    hb = TM_IN // HALO
    nhb = n // HALO
    const = lambda i: (0, 0)
    return pl.pallas_call(
        _ssd_in_kernel,
        grid=(nt,),
        in_specs=[
            pl.BlockSpec((HALO, D), lambda i: (jnp.maximum(i * hb - 1, 0), 0)),
            pl.BlockSpec((TM_IN, D), lambda i: (i, 0)),
            pl.BlockSpec((HALO, D), lambda i: (jnp.minimum((i + 1) * hb, nhb - 1), 0)),
            pl.BlockSpec((1, D), const),
            pl.BlockSpec((None, 1, D), lambda i: (i // tpb, 0, 0)),
            pl.BlockSpec((None, 1, D), lambda i: (i // tpb, 0, 1)),
            pl.BlockSpec((D, XBC), const),
            pl.BlockSpec((D, LANES), const),
            pl.BlockSpec((1, LANES), const),
            pl.BlockSpec((1, LANES), const),
            pl.BlockSpec((SSM_K, XBC), const),
            pl.BlockSpec((1, XBC), const),
        ],
        out_specs=[
            pl.BlockSpec((TM_IN // QC, INNER, QC), lambda i: (i, 0, 0)),
            pl.BlockSpec((TM_IN, 2 * GN), lambda i: (i, 0)),
            pl.BlockSpec((2, TM_IN, LANES), lambda i: (0, i, 0)),
        ],
        out_shape=[
            jax.ShapeDtypeStruct((n // QC, INNER, QC), BF16),
            jax.ShapeDtypeStruct((n, 2 * GN), BF16),
            jax.ShapeDtypeStruct((2, n, LANES), F32),
        ],
        scratch_shapes=[
            pltpu.VMEM((D // LANES, TM_IN, LANES), F32),
            pltpu.VMEM((TM_IN + 2 * HALO, D), BF16),
            pltpu.VMEM((2, TM_IN + 2 * HALO, CN_IN), F32),
            pltpu.VMEM((2, CN_IN // LANES, TM_IN, LANES), F32),
            pltpu.VMEM((2, TM_IN, LANES), F32),
        ],
        compiler_params=pltpu.CompilerParams(vmem_limit_bytes=VMEM_LIMIT),
        name="ssd_in",
    )(x2, x2, x2, nw, mod3, mod3, w_xbc, wdt, dtb, alog, cw, cb)


CTX_COLS = INNER + GN


def _ctx_kernel(c_ref, nw_ref, sh_ref, sc_ref, w_ref, wdt_ref, dtb_ref, alog_ref, cw_ref, cb_ref,
                h0_ref, p_scr, xb_scr):
    hc = _norm_mod(c_ref[...], nw_ref[...], sh_ref[...], sc_ref[...])
    hb = hc.astype(BF16)
    p_scr[pl.ds(0, HALO), :] = jnp.zeros((HALO, CN_IN), F32)
    p_scr[pl.ds(HALO + CTX, HALO), :] = jnp.zeros((HALO, CN_IN), F32)
    for c in range(CTX_COLS // CN_IN):
        cols = pl.ds(c * CN_IN, CN_IN)
        p_scr[pl.ds(HALO, CTX), :] = _dot(hb, w_ref[:, cols])
        xb_scr[:, cols] = _conv5_silu(p_scr, CTX, 1, cw_ref[:, cols], cb_ref[:, cols]).astype(BF16)

    val = _dt_lanes(hc, wdt_ref, dtb_ref, alog_ref)
    row = lax.broadcasted_iota(jnp.int32, (CTX, CTX), 0)
    col = lax.broadcasted_iota(jnp.int32, (CTX, CTX), 1)
    tri = (row >= col).astype(BF16)
    v_hi, v_mid, v_lo = _split3(val)
    cum = _dot(tri, v_hi) + _dot(tri, v_mid) + _dot(tri, v_lo)
    tot = cum[CTX - 1:CTX, :]
    dts = pltpu.roll(val, HEADS, axis=1)
    lane = lax.broadcasted_iota(jnp.int32, (1, LANES), 1)
    wgt = jnp.where(lane < 64, jnp.exp(tot - cum), jnp.exp(cum - val)) * dts
    lane2 = lax.broadcasted_iota(jnp.int32, (CTX, LANES), 1)
    for d in range(2):
        base = d * 64 + HEADS
        for pair in range(HEADS // 2):
            g = pair // 2
            la = base + 2 * pair
            wcol = jnp.where(lane2 < HEAD_DIM, wgt[:, la:la + 1], wgt[:, la + 1:la + 2])
            xs = (xb_scr[:, pl.ds(pair * LANES, LANES)].astype(F32) * wcol).astype(BF16)
            bg = xb_scr[:, pl.ds(INNER + g * STATE, STATE)]
            h0_ref[d, pl.ds(pair * LANES, LANES), :] = _dot_tn(xs, bg)


def _ctx(ctx, mod3, nw, w_xbc, wdt, dtb, alog, cw, cb):
    b = ctx.shape[0]
    const = lambda i: (0, 0)
    return pl.pallas_call(
        _ctx_kernel,
        grid=(b,),
        in_specs=[
            pl.BlockSpec((None, CTX, D), lambda i: (i, 0, 0)),
            pl.BlockSpec((1, D), const),
            pl.BlockSpec((None, 1, D), lambda i: (8, 0, 0)),
            pl.BlockSpec((None, 1, D), lambda i: (8, 0, 1)),
            pl.BlockSpec((D, CTX_COLS), const),
            pl.BlockSpec((D, LANES), const),
            pl.BlockSpec((1, LANES), const),
            pl.BlockSpec((1, LANES), const),
            pl.BlockSpec((SSM_K, CTX_COLS), const),
            pl.BlockSpec((1, CTX_COLS), const),
        ],
        out_specs=pl.BlockSpec((2, None, INNER, STATE), lambda i: (0, i, 0, 0)),
        out_shape=jax.ShapeDtypeStruct((2, b, INNER, STATE), F32),
        scratch_shapes=[
            pltpu.VMEM((CTX + 2 * HALO, CN_IN), F32),
            pltpu.VMEM((CTX, CTX_COLS), BF16),
        ],
        compiler_params=pltpu.CompilerParams(vmem_limit_bytes=VMEM_LIMIT),
        name="ctx",
    )(ctx, nw, mod3, mod3, w_xbc, wdt, dtb, alog, cw, cb)


def _scan_kernel(d_smem, xt_ref, b_ref, c_ref, dt_ref, h0_ref, y_ref, st, *, bwd):
    c = pl.program_id(1)

    @pl.when(c == 0)
    def _():
        st[...] = h0_ref[...]

    q = QC
    hq = q // 2
    dtb = dt_ref[...]
    row = lax.broadcasted_iota(jnp.int32, (q, q), 0)
    col = lax.broadcasted_iota(jnp.int32, (q, q), 1)
    tri = ((col >= row) if bwd else (row >= col)).astype(BF16)
    v_hi, v_mid, v_lo = _split3(dtb)
    cum = (_dot(tri, v_hi) + _dot(tri, v_mid) + _dot(tri, v_lo)) * LOG2E
    cum_t = cum.T[HEADS:2 * HEADS, :]
    dt_t = dtb.T[0:HEADS, :]
    tot_t = cum_t[:, 0:1] if bwd else cum_t[:, q - 1:q]
    ecum_t = jnp.exp2(cum_t)
    dte_t = jnp.exp2(tot_t - cum_t) * dt_t
    etot = jnp.exp2(tot_t)
    keep = ((row >= col) if bwd else (col >= row)).astype(F32)
    fr = pl.ds(hq, hq) if bwd else pl.ds(0, hq)
    pr = pl.ds(0, hq) if bwd else pl.ds(hq, hq)
    fs = slice(hq, q) if bwd else slice(0, hq)
    ps = slice(0, hq) if bwd else slice(hq, q)
    zero = jnp.zeros((hq, hq), BF16)

    gp = (HEADS // GROUPS) * HEAD_DIM
    hpg = HEADS // GROUPS

    def prepare(g):
        cg = c_ref[:, pl.ds(g * STATE, STATE)]
        s_old = st[pl.ds(g * gp, gp), :]
        both = _dot_nt(jnp.concatenate([b_ref[fr, pl.ds(g * STATE, STATE)], s_old.astype(BF16)], axis=0), cg)
        cb_diag = (both[0:hq, fs] * keep[fs, fs]).astype(BF16)
        cb_off = both[0:hq, ps].astype(BF16)
        cbp = (_dot_nt(b_ref[pr, pl.ds(g * STATE, STATE)], c_ref[pr, pl.ds(g * STATE, STATE)])
               * keep[ps, ps]).astype(BF16)
        mts = []
        for j in range(hpg):
            h = g * hpg + j
            crow = cum_t[h:h + 1, :]
            ccol = cum[:, HEADS + h:HEADS + h + 1]
            m_diag = cb_diag * jnp.exp2(jnp.minimum(crow[:, fs] - ccol[fs], 0.0)).astype(BF16)
            m_off = cb_off * jnp.exp2(crow[:, ps] - ccol[fs]).astype(BF16)
            mp = cbp * jnp.exp2(jnp.minimum(crow[:, ps] - ccol[ps], 0.0)).astype(BF16)
            if bwd:
                mts.append(jnp.concatenate([jnp.concatenate([mp, zero], axis=1),
                                            jnp.concatenate([m_off, m_diag], axis=1)], axis=0))
            else:
                mts.append(jnp.concatenate([jnp.concatenate([m_diag, m_off], axis=1),
                                            jnp.concatenate([zero, mp], axis=1)], axis=0))
        return s_old, both[hq:hq + gp], mts

    def finish(g, s_old, y_off, mts):
        xs, cds = [], []
        for j in range(hpg):
            h = g * hpg + j
            r0 = g * gp + j * HEAD_DIM
            xb = xt_ref[pl.ds(r0, HEAD_DIM), :]
            y_diag = _dot(xb * jnp.broadcast_to(dt_t[h:h + 1, :], xb.shape).astype(BF16), mts[j])
            yh = y_diag + y_off[j * HEAD_DIM:(j + 1) * HEAD_DIM] * ecum_t[h:h + 1, :]
            if not bwd:
                yh = yh + d_smem[h] * xb.astype(F32)
            y_ref[pl.ds(r0, HEAD_DIM), :] = yh.astype(BF16)
            xs.append(xb * jnp.broadcast_to(dte_t[h:h + 1, :], xb.shape).astype(BF16))
            cds.append(jnp.broadcast_to(etot[h:h + 1, :], (HEAD_DIM, STATE)))
        s_new = _dot(jnp.concatenate(xs, axis=0), b_ref[:, pl.ds(g * STATE, STATE)])
        st[pl.ds(g * gp, gp), :] = s_old * jnp.concatenate(cds, axis=0) + s_new

    nxt = prepare(0)
    for g in range(GROUPS):
        cur = nxt
        if g + 1 < GROUPS:
            nxt = prepare(g + 1)
        finish(g, *cur)


def _scan(bwd, ssm_d, xt, bc, dtda, h0):
    n = bc.shape[0]
    bsz = n // SEQ
    nc = SEQ // QC
    d = int(bwd)

    def chunk(b, c):
        return b * nc + (nc - 1 - c if bwd else c)

    return pl.pallas_call(
        functools.partial(_scan_kernel, bwd=bwd),
        grid=(bsz, nc),
        in_specs=[
            pl.BlockSpec(memory_space=pltpu.SMEM),
            pl.BlockSpec((None, INNER, QC), lambda b, c: (chunk(b, c), 0, 0)),
            pl.BlockSpec((QC, GN), lambda b, c: (chunk(b, c), 0)),
            pl.BlockSpec((QC, GN), lambda b, c: (chunk(b, c), 1)),
            pl.BlockSpec((None, QC, LANES), lambda b, c: (d, chunk(b, c), 0)),
            pl.BlockSpec((None, None, INNER, STATE), lambda b, c: (d, b, 0, 0)),
        ],
        out_specs=pl.BlockSpec((None, INNER, QC), lambda b, c: (chunk(b, c), 0, 0)),
        out_shape=jax.ShapeDtypeStruct((n // QC, INNER, QC), BF16),
        scratch_shapes=[pltpu.VMEM((INNER, STATE), F32)],
        compiler_params=pltpu.CompilerParams(
            dimension_semantics=("arbitrary", "arbitrary"),
            vmem_limit_bytes=VMEM_LIMIT),
        name="scan_bwd" if bwd else "scan_fwd",
    )(ssm_d, xt, bc, bc, dtda, h0)


NSEG_CONV = TM_CONV // GRID_W
assert NSEG_CONV == SUB
PAD_CONV = (CONV_K // 2) * NSEG_CONV
RB_CONV = 64
MM_CONV = 256


def _conv_kernel(x_ref, nw_ref, sh_ref, sc_ref, wglu_ref, wg_ref, dw_ref, db_ref, lnw_ref, lnb_ref,
                 wco_ref, bco_ref, a_ref, xs_scr, pad_scr, u_scr):
    _to_seg_major(xs_scr, x_ref, NSEG_CONV, GRID_W)
    h = _norm_mod(_load_cols(xs_scr), nw_ref[...], sh_ref[...], sc_ref[...]).astype(BF16)
    zeros = jnp.zeros((PAD_CONV, LANES), F32)
    per_mm = MM_CONV // LANES
    def glu(cb):
        val = _dot(h, wglu_ref[:, pl.ds(cb * MM_CONV, MM_CONV)])
        gate = _dot(h, wglu_ref[:, pl.ds(D + cb * MM_CONV, MM_CONV)])
        return val * _sigmoid(gate)

    u_next = glu(0)
    for cb in range(D // MM_CONV):
        u = u_next
        if cb + 1 < D // MM_CONV:
            u_next = glu(cb + 1)
        for j in range(per_mm):
            cc = cb * per_mm + j
            pad_scr[cc, pl.ds(0, PAD_CONV), :] = zeros
            pad_scr[cc, pl.ds(PAD_CONV, TM_CONV), :] = u[:, j * LANES:(j + 1) * LANES]
            pad_scr[cc, pl.ds(PAD_CONV + TM_CONV, PAD_CONV), :] = zeros
            for rb in range(TM_CONV // RB_CONV):
                r0 = rb * RB_CONV
                acc = jnp.broadcast_to(db_ref[cc], (RB_CONV, LANES))
                for k in range(CONV_K):
                    acc = acc + pad_scr[cc, pl.ds(r0 + k * NSEG_CONV, RB_CONV), :] * dw_ref[cc, pl.ds(k, 1), :]
                u_scr[cc, pl.ds(r0, RB_CONV), :] = acc

    v = _load_cols(u_scr)
    mu = jnp.mean(v, axis=-1, keepdims=True)
    vc = v - mu
    var = jnp.mean(vc * vc, axis=-1, keepdims=True)
    ln = vc * lax.rsqrt(var + EPS) * lnw_ref[...] + lnb_ref[...]
    act = _silu(ln).astype(BF16)
    u_conv = _dot(act, wco_ref[...]) + bco_ref[...]
    g_conv = _sigmoid(_dot(h, wg_ref[...]))
    _store_cols(u_scr, g_conv * u_conv)
    _from_seg_major(a_ref, 0, u_scr, NSEG_CONV, GRID_W)


def _conv_branch(x2, mod3, nw, w_glu, w_gc, dw, db, lnw, lnb, wco, bco):
    n = x2.shape[0]
    tpb = SEQ // TM_CONV
    const = lambda i: (0, 0)
    return pl.pallas_call(
        _conv_kernel,
        grid=(n // TM_CONV,),
        in_specs=[
            pl.BlockSpec((TM_CONV, D), lambda i: (i, 0)),
            pl.BlockSpec((1, D), const),
            pl.BlockSpec((None, 1, D), lambda i: (i // tpb, 0, 0)),
            pl.BlockSpec((None, 1, D), lambda i: (i // tpb, 0, 1)),
            pl.BlockSpec((D, 2 * D), const),
            pl.BlockSpec((D, D), const),
            pl.BlockSpec((D // LANES, CONV_K, LANES), lambda i: (0, 0, 0)),
            pl.BlockSpec((D // LANES, 1, LANES), lambda i: (0, 0, 0)),
            pl.BlockSpec((1, D), const),
            pl.BlockSpec((1, D), const),
            pl.BlockSpec((D, D), const),
            pl.BlockSpec((1, D), const),
        ],
        out_specs=pl.BlockSpec((TM_CONV, D), lambda i: (i, 0)),
        out_shape=jax.ShapeDtypeStruct((n, D), BF16),
        scratch_shapes=[
            pltpu.VMEM((D // LANES, TM_CONV, LANES), F32),
            pltpu.VMEM((D // LANES, TM_CONV + 2 * PAD_CONV, LANES), F32),
            pltpu.VMEM((D // LANES, TM_CONV, LANES), F32),
        ],
        compiler_params=pltpu.CompilerParams(vmem_limit_bytes=VMEM_LIMIT),
        name="conv",
    )(x2, nw, mod3, mod3, w_glu, w_gc, dw, db, lnw, lnb, wco, bco)


def _merge_kernel(x_ref, yf_ref, yb_ref, a_ref, nw_ref, sh_ref, sc_ref, gt_ref, wz_ref, wg_ref,
                  snw_ref, wso_ref, wo_ref, o_ref):
    x = x_ref[...]
    h = _norm_mod(x, nw_ref[...], sh_ref[...], sc_ref[...]).astype(BF16)
    z = _dot(h, wz_ref[...])
    y = (yf_ref[...].astype(F32) + yb_ref[...].astype(F32)).T
    y = y * _silu(z)
    ms = jnp.mean(y * y, axis=-1, keepdims=True)
    yn = (y * lax.rsqrt(ms + EPS) * snw_ref[...]).astype(BF16)
    u_ssd = _dot(yn, wso_ref[...])
    g_ssd = _sigmoid(_dot(h, wg_ref[...]))
    m = a_ref[...].astype(F32) + g_ssd * u_ssd
    mix = _dot(m.astype(BF16), wo_ref[...])
    o_ref[...] = x + gt_ref[...] * mix


def _merge(x2, yf, yb, a, mod3, nw, w_z, w_gs, snw, wso, wo):
    n = x2.shape[0]
    tm = TM_MERGE
    assert tm == QC
    tpb = SEQ // tm
    const = lambda i: (0, 0)
    return pl.pallas_call(
        _merge_kernel,
        grid=(n // tm,),
        in_specs=[
            pl.BlockSpec((tm, D), lambda i: (i, 0)),
            pl.BlockSpec((None, INNER, QC), lambda i: (i, 0, 0)),
            pl.BlockSpec((None, INNER, QC), lambda i: (i, 0, 0)),
            pl.BlockSpec((tm, D), lambda i: (i, 0)),
            pl.BlockSpec((1, D), const),
            pl.BlockSpec((None, 1, D), lambda i: (i // tpb, 0, 0)),
            pl.BlockSpec((None, 1, D), lambda i: (i // tpb, 0, 1)),
            pl.BlockSpec((None, 1, D), lambda i: (i // tpb, 0, 2)),
            pl.BlockSpec((D, INNER), const),
            pl.BlockSpec((D, D), const),
            pl.BlockSpec((1, INNER), const),
            pl.BlockSpec((INNER, D), const),
            pl.BlockSpec((D, D), const),
        ],
        out_specs=pl.BlockSpec((tm, D), lambda i: (i, 0)),
        out_shape=jax.ShapeDtypeStruct((n, D), F32),
        compiler_params=pltpu.CompilerParams(vmem_limit_bytes=VMEM_LIMIT),
        name="merge",
    )(x2, yf, yb, a, nw, mod3, mod3, mod3, w_z, w_gs, snw, wso, wo)


def _mlp_kernel(x_ref, nw_ref, sh_ref, sc_ref, gt_ref, w1_ref, w2_ref, fnw_ref, o_ref):
    x = x_ref[...]
    h = _norm_mod(x, nw_ref[...], sh_ref[...], sc_ref[...]).astype(BF16)
    acc = jnp.zeros((TM_MLP, D), F32)
    for c in range(D_FF // FF_CHUNK):
        t = jnp.maximum(_dot(h, w1_ref[:, pl.ds(c * FF_CHUNK, FF_CHUNK)]), 0.0)
        acc = acc + _dot((t * t).astype(BF16), w2_ref[pl.ds(c * FF_CHUNK, FF_CHUNK), :])
    x2 = x + gt_ref[...] * acc
    ms = jnp.mean(x2 * x2, axis=-1, keepdims=True)
    o_ref[...] = x2 * lax.rsqrt(ms + EPS) * fnw_ref[...]


def _mlp(x1, mod3, nw, w1, w2, fnw):
    n = x1.shape[0]
    tm = TM_MLP
    tpb = SEQ // tm
    const = lambda i: (0, 0)
    return pl.pallas_call(
        _mlp_kernel,
        grid=(n // tm,),
        in_specs=[
            pl.BlockSpec((tm, D), lambda i: (i, 0)),
            pl.BlockSpec((1, D), const),
            pl.BlockSpec((None, 1, D), lambda i: (i // tpb, 0, 3)),
            pl.BlockSpec((None, 1, D), lambda i: (i // tpb, 0, 4)),
            pl.BlockSpec((None, 1, D), lambda i: (i // tpb, 0, 5)),
            pl.BlockSpec((D, D_FF), const),
            pl.BlockSpec((D_FF, D), const),
            pl.BlockSpec((1, D), const),
        ],
        out_specs=pl.BlockSpec((tm, D), lambda i: (i, 0)),
        out_shape=jax.ShapeDtypeStruct((n, D), F32),
        compiler_params=pltpu.CompilerParams(vmem_limit_bytes=VMEM_LIMIT),
        name="mlp",
    )(x1, nw, mod3, mod3, mod3, w1, w2, fnw)


def kernel(x, c, ctx, c_ctx, w_ada, b_ada, norm1_w, norm2_w, w_in, conv_dw_w, conv_dw_b, conv_ln_w,
           conv_ln_b, w_conv_out, b_conv_out, ssm_conv_w, ssm_conv_b, ssm_dt_bias, ssm_a_log, ssm_d,
           ssm_norm_w, w_ssm_out, w_o, w_mlp1, w_mlp2, final_norm_w):
    bsz, seq, _ = x.shape
    assert (bsz, seq, x.shape[2]) == (8, SEQ, D) and ctx.shape[1] == CTX and w_ada.shape[0] == 1
    n = bsz * seq
    x2 = x.reshape(n, D)
    row = lambda v: v.reshape(1, -1)

    wi = w_in[0]
    w_xbc = wi[:, :COL_DT].astype(BF16)
    wdt_f = wi[:, COL_DT:COL_DT + HEADS]
    wdt_b = wi[:, COL_DT + HEADS:COL_Z]
    wdt = jnp.concatenate([wdt_f, wdt_f, wdt_b, wdt_b], axis=1)
    bf_, bb_ = ssm_dt_bias[0, 0], ssm_dt_bias[0, 1]
    dtb = row(jnp.concatenate([bf_, bf_, bb_, bb_]))
    zh = jnp.zeros((HEADS,), F32)
    alog = row(jnp.concatenate([zh, ssm_a_log[0, 0], zh, ssm_a_log[0, 1]]))
    w_z = wi[:, COL_Z:COL_GLU].astype(BF16)
    w_glu = wi[:, COL_GLU:COL_GATE].astype(BF16)
    w_gc = wi[:, COL_GATE:COL_GATE + D].astype(BF16)
    w_gs = wi[:, COL_GATE + D:].astype(BF16)

    cond = jnp.concatenate([c, c_ctx[None, :], jnp.zeros((16 - bsz - 1, D), F32)], axis=0)
    mod = _ada(cond, w_ada[0], row(b_ada[0]))
    mod3 = mod.reshape(16, 1, 6 * D)

    nw1 = row(norm1_w[0])
    cw, cb = ssm_conv_w[0], row(ssm_conv_b[0])
    h0 = _ctx(ctx, mod3, nw1, w_xbc, wdt, dtb, alog, cw, cb)
    xt, bc, dtda = _ssd_in(x2, mod3, nw1, w_xbc, wdt, dtb, alog, cw, cb)
    yf = _scan(False, ssm_d[0], xt, bc, dtda, h0)
    yb = _scan(True, ssm_d[0], xt, bc, dtda, h0)
    dw3 = conv_dw_w[0].reshape(CONV_K, D // LANES, LANES).transpose(1, 0, 2)
    db3 = conv_dw_b[0].reshape(D // LANES, 1, LANES)
    a = _conv_branch(x2, mod3, nw1, w_glu, w_gc, dw3, db3, row(conv_ln_w[0]),
                     row(conv_ln_b[0]), w_conv_out[0].astype(BF16), row(b_conv_out[0]))
    x1 = _merge(x2, yf, yb, a, mod3, nw1, w_z, w_gs, row(ssm_norm_w[0]),
                w_ssm_out[0].astype(BF16), w_o[0].astype(BF16))
    out = _mlp(x1, mod3, row(norm2_w[0]), w_mlp1[0].astype(BF16), w_mlp2[0].astype(BF16),
               row(final_norm_w))
    return out.reshape(bsz, seq, D)
```

```python
import functools

import jax
import jax.numpy as jnp
from jax import lax
from jax.experimental import pallas as pl
from jax.experimental.pallas import tpu as pltpu

F32 = jnp.float32
BF16 = jnp.bfloat16

D = 1024
SEQ = 4096
CTX = 256
GRID_W = 64
CONV_K = 31
INNER = 2048
HEADS = 32
HEAD_DIM = 64
GROUPS = 8
STATE = 128
GN = GROUPS * STATE
SSM_K = 5
CHUNK = 128
XBC = INNER + 2 * GN
D_FF = 4 * D
COL_DT = XBC
COL_Z = COL_DT + 2 * HEADS
COL_GLU = COL_Z + INNER
COL_GATE = COL_GLU + 2 * D
EPS = 1e-6

LANES = 128
HALO = 16
VMEM_LIMIT = 56 * 1024 * 1024

QC = 256
TM_IN = 512
CN_IN = 512
TM_CONV = 512
TM_MERGE = 256
TM_MLP = 512
FF_CHUNK = 1024


def _dot(a, b):
    return jnp.dot(a, b, preferred_element_type=F32)


def _dot_nt(a, b):
    return lax.dot_general(a, b, (((1,), (1,)), ((), ())), preferred_element_type=F32)


def _dot_tn(a, b):
    return lax.dot_general(a, b, (((0,), (0,)), ((), ())), preferred_element_type=F32)


LOG2E = 1.4426950408889634


def _sigmoid(v):
    return 1.0 / (1.0 + jnp.exp2(v * (-LOG2E)))


def _silu(v):
    return v * _sigmoid(v)


def _norm_mod(x, nw, shift, scale):
    ms = jnp.mean(x * x, axis=-1, keepdims=True)
    return x * lax.rsqrt(ms + EPS) * (nw * (1.0 + scale)) + shift


def _split3(v):
    hi = v.astype(BF16)
    r1 = v - hi.astype(F32)
    mid = r1.astype(BF16)
    lo = (r1 - mid.astype(F32)).astype(BF16)
    return hi, mid, lo


def _dt_lanes(h, wdt_ref, dtb_ref, alog_ref):
    w = wdt_ref[...]
    w_hi = w.astype(BF16)
    w_lo = (w - w_hi.astype(F32)).astype(BF16)
    h_hi = h.astype(BF16)
    h_lo = (h - h_hi.astype(F32)).astype(BF16)
    both = _dot(h_hi, jnp.concatenate([w_hi, w_lo], axis=1))
    raw = both[:, :LANES] + both[:, LANES:] + _dot(h_lo, w_hi)
    v = raw + dtb_ref[...]
    dt = jnp.maximum(v, 0.0) + jnp.log1p(jnp.exp(-jnp.abs(v)))
    lane = lax.broadcasted_iota(jnp.int32, (1, LANES), 1)
    mult = jnp.where((lane % 64) < HEADS, 1.0, -jnp.exp(alog_ref[...]))
    return dt * mult


def _ada_kernel(cond_ref, w_ref, b_ref, o_ref):
    c = cond_ref[...]
    s = _silu(c)
    o_ref[...] = _dot(s.astype(BF16), w_ref[...].astype(BF16)) + b_ref[...]


def _ada(cond, w, b):
    n = w.shape[1]
    tn = 1024
    return pl.pallas_call(
        _ada_kernel,
        grid=(n // tn,),
        in_specs=[
            pl.BlockSpec((16, D), lambda j: (0, 0)),
            pl.BlockSpec((D, tn), lambda j: (0, j)),
            pl.BlockSpec((1, tn), lambda j: (0, j)),
        ],
        out_specs=pl.BlockSpec((16, tn), lambda j: (0, j)),
        out_shape=jax.ShapeDtypeStruct((16, n), F32),
        name="ada",
    )(cond, w, b)


def _conv5_silu(p_scr, rows, step, cw, cb):
    acc = cb
    for k in range(SSM_K):
        acc = acc + p_scr[pl.ds(HALO - (SSM_K // 2) * step + k * step, rows), :] * cw[k:k + 1, :]
    return _silu(acc)


SUB = 8
NSEG_IN = SUB
SEG_IN = TM_IN // NSEG_IN


def _to_seg_major(dst, src, nseg, seg):
    for j in range(dst.shape[0]):
        for s in range(nseg):
            dst[j, pl.ds(s, seg, stride=nseg), :] = src[pl.ds(s * seg, seg), pl.ds(j * LANES, LANES)]


def _from_seg_major(dst, col0, src, nseg, seg):
    for j in range(src.shape[0]):
        for s in range(nseg):
            dst[pl.ds(s * seg, seg), pl.ds(col0 + j * LANES, LANES)] = (
                src[j, pl.ds(s, seg, stride=nseg), :].astype(dst.dtype))


def _load_cols(ref):
    return jnp.concatenate([ref[j] for j in range(ref.shape[0])], axis=1)


def _store_cols(ref, v):
    for j in range(ref.shape[0]):
        ref[j] = v[:, j * LANES:(j + 1) * LANES]


def _ssd_in_kernel(xp_ref, x_ref, xn_ref, nw_ref, sh_ref, sc_ref, w_ref, wdt_ref, dtb_ref,
                   alog_ref, cw_ref, cb_ref, xt_ref, bc_ref, dt_ref, xs_scr, h_scr, p_scr, r_scr, v_scr):
    i = pl.program_id(0)
    tpb = SEQ // TM_IN
    first = (i % tpb) == 0
    last = (i % tpb) == tpb - 1
    nw, sh, sc = nw_ref[...], sh_ref[...], sc_ref[...]
    _to_seg_major(xs_scr, x_ref, NSEG_IN, SEG_IN)
    hm = _norm_mod(_load_cols(xs_scr), nw, sh, sc)
    h_scr[pl.ds(0, TM_IN), :] = hm.astype(BF16)
    h_scr[pl.ds(TM_IN, HALO), :] = _norm_mod(xp_ref[...], nw, sh, sc).astype(BF16)
    h_scr[pl.ds(TM_IN + HALO, HALO), :] = _norm_mod(xn_ref[...], nw, sh, sc).astype(BF16)

    val = _dt_lanes(hm, wdt_ref, dtb_ref, alog_ref)
    lane = lax.broadcasted_iota(jnp.int32, (1, LANES), 1)
    v_scr[0] = jnp.where(lane < 64, val, 0.0)
    v_scr[1] = jnp.where(lane < 64, pltpu.roll(val, 64, axis=1), 0.0)
    for d in range(2):
        for s in range(NSEG_IN):
            dt_ref[d, pl.ds(s * SEG_IN, SEG_IN), :] = v_scr[d, pl.ds(s, SEG_IN, stride=NSEG_IN), :]

    sub = lax.broadcasted_iota(jnp.int32, (SUB, CN_IN), 0)
    keep_prev = jnp.where(first, 0.0, 1.0)
    keep_next = jnp.where(last, 0.0, 1.0)
    nblk = TM_IN // SUB
    for c in range(XBC // CN_IN):
        cols = pl.ds(c * CN_IN, CN_IN)
        p = _dot(h_scr[...], w_ref[:, cols])
        pp = p_scr.at[c % 2]
        rr = r_scr.at[c % 2]
        pp[pl.ds(HALO, TM_IN), :] = p[0:TM_IN]
        prev = p[TM_IN + HALO - SUB:TM_IN + HALO] * keep_prev
        nxt = p[TM_IN + HALO:TM_IN + HALO + SUB] * keep_next
        blk = lambda b: p[b * SUB:(b + 1) * SUB]
        pp[pl.ds(HALO - SUB, SUB), :] = jnp.where(
            sub == 0, pltpu.roll(prev, 1, axis=0), pltpu.roll(blk(nblk - 1), 1, axis=0))
        pp[pl.ds(HALO - 2 * SUB, SUB), :] = jnp.where(
            sub == 0, pltpu.roll(prev, 2, axis=0), pltpu.roll(blk(nblk - 2), 1, axis=0))
        pp[pl.ds(HALO + TM_IN, SUB), :] = jnp.where(
            sub == SUB - 1, pltpu.roll(nxt, SUB - 1, axis=0), pltpu.roll(blk(0), SUB - 1, axis=0))
        pp[pl.ds(HALO + TM_IN + SUB, SUB), :] = jnp.where(
            sub == SUB - 1, pltpu.roll(nxt, SUB - 2, axis=0), pltpu.roll(blk(1), SUB - 1, axis=0))
        _store_cols(rr, _conv5_silu(pp, TM_IN, SUB, cw_ref[:, cols], cb_ref[:, cols]))
        if c * CN_IN >= INNER:
            _from_seg_major(bc_ref, c * CN_IN - INNER, rr, NSEG_IN, SEG_IN)
        else:
            for j in range(CN_IN // LANES):
                for m in range(TM_IN // LANES):
                    tok = jnp.concatenate(
                        [rr[j, pl.ds(2 * m + e, SEG_IN, stride=NSEG_IN), :] for e in range(LANES // SEG_IN)],
                        axis=0)
                    xt_ref[(m * LANES) // QC, pl.ds(c * CN_IN + j * LANES, LANES),
                           pl.ds((m * LANES) % QC, LANES)] = tok.T.astype(BF16)


def _ssd_in(x2, mod3, nw, w_xbc, wdt, dtb, alog, cw, cb):
    n = x2.shape[0]
    nt = n // TM_IN
    tpb = SEQ // TM_IN
    hb = TM_IN // HALO
    nhb = n // HALO
    const = lambda i: (0, 0)
    return pl.pallas_call(
        _ssd_in_kernel,
        grid=(nt,),
        in_specs=[
            pl.BlockSpec((HALO, D), lambda i: (jnp.maximum(i * hb - 1, 0), 0)),
            pl.BlockSpec((TM_IN, D), lambda i: (i, 0)),
            pl.BlockSpec((HALO, D), lambda i: (jnp.minimum((i + 1) * hb, nhb - 1), 0)),
            pl.BlockSpec((1, D), const),
            pl.BlockSpec((None, 1, D), lambda i: (i // tpb, 0, 0)),
            pl.BlockSpec((None, 1, D), lambda i: (i // tpb, 0, 1)),
            pl.BlockSpec((D, XBC), const),
            pl.BlockSpec((D, LANES), const),
            pl.BlockSpec((1, LANES), const),
            pl.BlockSpec((1, LANES), const),
            pl.BlockSpec((SSM_K, XBC), const),
            pl.BlockSpec((1, XBC), const),
        ],
        out_specs=[
            pl.BlockSpec((TM_IN // QC, INNER, QC), lambda i: (i, 0, 0)),
            pl.BlockSpec((TM_IN, 2 * GN), lambda i: (i, 0)),
            pl.BlockSpec((2, TM_IN, LANES), lambda i: (0, i, 0)),
        ],
        out_shape=[
            jax.ShapeDtypeStruct((n // QC, INNER, QC), BF16),
            jax.ShapeDtypeStruct((n, 2 * GN), BF16),
            jax.ShapeDtypeStruct((2, n, LANES), F32),
        ],
        scratch_shapes=[
            pltpu.VMEM((D // LANES, TM_IN, LANES), F32),
            pltpu.VMEM((TM_IN + 2 * HALO, D), BF16),
            pltpu.VMEM((2, TM_IN + 2 * HALO, CN_IN), F32),
            pltpu.VMEM((2, CN_IN // LANES, TM_IN, LANES), F32),
            pltpu.VMEM((2, TM_IN, LANES), F32),
        ],
        compiler_params=pltpu.CompilerParams(vmem_limit_bytes=VMEM_LIMIT),
        name="ssd_in",
    )(x2, x2, x2, nw, mod3, mod3, w_xbc, wdt, dtb, alog, cw, cb)


CTX_COLS = INNER + GN


def _ctx_kernel(c_ref, nw_ref, sh_ref, sc_ref, w_ref, wdt_ref, dtb_ref, alog_ref, cw_ref, cb_ref,
                h0_ref, p_scr, xb_scr):
    hc = _norm_mod(c_ref[...], nw_ref[...], sh_ref[...], sc_ref[...])
    hb = hc.astype(BF16)
    p_scr[pl.ds(0, HALO), :] = jnp.zeros((HALO, CN_IN), F32)
    p_scr[pl.ds(HALO + CTX, HALO), :] = jnp.zeros((HALO, CN_IN), F32)
    for c in range(CTX_COLS // CN_IN):
        cols = pl.ds(c * CN_IN, CN_IN)
        p_scr[pl.ds(HALO, CTX), :] = _dot(hb, w_ref[:, cols])
        xb_scr[:, cols] = _conv5_silu(p_scr, CTX, 1, cw_ref[:, cols], cb_ref[:, cols]).astype(BF16)

    val = _dt_lanes(hc, wdt_ref, dtb_ref, alog_ref)
    row = lax.broadcasted_iota(jnp.int32, (CTX, CTX), 0)
    col = lax.broadcasted_iota(jnp.int32, (CTX, CTX), 1)
    tri = (row >= col).astype(BF16)
    v_hi, v_mid, v_lo = _split3(val)
    cum = _dot(tri, v_hi) + _dot(tri, v_mid) + _dot(tri, v_lo)
    tot = cum[CTX - 1:CTX, :]
    dts = pltpu.roll(val, HEADS, axis=1)
    lane = lax.broadcasted_iota(jnp.int32, (1, LANES), 1)
    wgt = jnp.where(lane < 64, jnp.exp(tot - cum), jnp.exp(cum - val)) * dts
    lane2 = lax.broadcasted_iota(jnp.int32, (CTX, LANES), 1)
    for d in range(2):
        base = d * 64 + HEADS
        for pair in range(HEADS // 2):
            g = pair // 2
            la = base + 2 * pair
            wcol = jnp.where(lane2 < HEAD_DIM, wgt[:, la:la + 1], wgt[:, la + 1:la + 2])
            xs = (xb_scr[:, pl.ds(pair * LANES, LANES)].astype(F32) * wcol).astype(BF16)
            bg = xb_scr[:, pl.ds(INNER + g * STATE, STATE)]
            h0_ref[d, pl.ds(pair * LANES, LANES), :] = _dot_tn(xs, bg)


def _ctx(ctx, mod3, nw, w_xbc, wdt, dtb, alog, cw, cb):
    b = ctx.shape[0]
    const = lambda i: (0, 0)
    return pl.pallas_call(
        _ctx_kernel,
        grid=(b,),
        in_specs=[
            pl.BlockSpec((None, CTX, D), lambda i: (i, 0, 0)),
            pl.BlockSpec((1, D), const),
            pl.BlockSpec((None, 1, D), lambda i: (8, 0, 0)),
            pl.BlockSpec((None, 1, D), lambda i: (8, 0, 1)),
            pl.BlockSpec((D, CTX_COLS), const),
            pl.BlockSpec((D, LANES), const),
            pl.BlockSpec((1, LANES), const),
            pl.BlockSpec((1, LANES), const),
            pl.BlockSpec((SSM_K, CTX_COLS), const),
            pl.BlockSpec((1, CTX_COLS), const),
        ],
        out_specs=pl.BlockSpec((2, None, INNER, STATE), lambda i: (0, i, 0, 0)),
        out_shape=jax.ShapeDtypeStruct((2, b, INNER, STATE), F32),
        scratch_shapes=[
            pltpu.VMEM((CTX + 2 * HALO, CN_IN), F32),
            pltpu.VMEM((CTX, CTX_COLS), BF16),
        ],
        compiler_params=pltpu.CompilerParams(vmem_limit_bytes=VMEM_LIMIT),
        name="ctx",
    )(ctx, nw, mod3, mod3, w_xbc, wdt, dtb, alog, cw, cb)


def _scan_dir(d_smem, xt_ref, b_ref, c_ref, dt_ref, y_ref, st, bwd):
    q = QC
    hq = q // 2
    dtb = dt_ref[...]
    row = lax.broadcasted_iota(jnp.int32, (q, q), 0)
    col = lax.broadcasted_iota(jnp.int32, (q, q), 1)
    tri = ((col >= row) if bwd else (row >= col)).astype(BF16)
    v_hi, v_mid, v_lo = _split3(dtb)
    cum = (_dot(tri, v_hi) + _dot(tri, v_mid) + _dot(tri, v_lo)) * LOG2E
    cum_t = cum.T[HEADS:2 * HEADS, :]
    dt_t = dtb.T[0:HEADS, :]
    tot_t = cum_t[:, 0:1] if bwd else cum_t[:, q - 1:q]
    ecum_t = jnp.exp2(cum_t)
    dte_t = jnp.exp2(tot_t - cum_t) * dt_t
    etot = jnp.exp2(tot_t)
    keep = ((row >= col) if bwd else (col >= row)).astype(F32)
    fr = pl.ds(hq, hq) if bwd else pl.ds(0, hq)
    pr = pl.ds(0, hq) if bwd else pl.ds(hq, hq)
    fs = slice(hq, q) if bwd else slice(0, hq)
    ps = slice(0, hq) if bwd else slice(hq, q)
    zero = jnp.zeros((hq, hq), BF16)

    gp = (HEADS // GROUPS) * HEAD_DIM
    hpg = HEADS // GROUPS

    def prepare(g):
        cg = c_ref[:, pl.ds(g * STATE, STATE)]
        s_old = st[pl.ds(g * gp, gp), :]
        both = _dot_nt(jnp.concatenate([b_ref[fr, pl.ds(g * STATE, STATE)], s_old.astype(BF16)], axis=0), cg)
        cb_diag = (both[0:hq, fs] * keep[fs, fs]).astype(BF16)
        cb_off = both[0:hq, ps].astype(BF16)
        cbp = (_dot_nt(b_ref[pr, pl.ds(g * STATE, STATE)], c_ref[pr, pl.ds(g * STATE, STATE)])
               * keep[ps, ps]).astype(BF16)
        mts = []
        for j in range(hpg):
            h = g * hpg + j
            crow = cum_t[h:h + 1, :]
            ccol = cum[:, HEADS + h:HEADS + h + 1]
            m_diag = cb_diag * jnp.exp2(jnp.minimum(crow[:, fs] - ccol[fs], 0.0)).astype(BF16)
            m_off = cb_off * jnp.exp2(crow[:, ps] - ccol[fs]).astype(BF16)
            mp = cbp * jnp.exp2(jnp.minimum(crow[:, ps] - ccol[ps], 0.0)).astype(BF16)
            if bwd:
                mts.append(jnp.concatenate([jnp.concatenate([mp, zero], axis=1),
                                            jnp.concatenate([m_off, m_diag], axis=1)], axis=0))
            else:
                mts.append(jnp.concatenate([jnp.concatenate([m_diag, m_off], axis=1),
                                            jnp.concatenate([zero, mp], axis=1)], axis=0))
        return s_old, both[hq:hq + gp], mts

    def finish(g, s_old, y_off, mts):
        xs, cds = [], []
        for j in range(hpg):
            h = g * hpg + j
            r0 = g * gp + j * HEAD_DIM
            xb = xt_ref[pl.ds(r0, HEAD_DIM), :]
            y_diag = _dot(xb * jnp.broadcast_to(dt_t[h:h + 1, :], xb.shape).astype(BF16), mts[j])
            yh = y_diag + y_off[j * HEAD_DIM:(j + 1) * HEAD_DIM] * ecum_t[h:h + 1, :]
            if not bwd:
                yh = yh + d_smem[h] * xb.astype(F32)
            y_ref[pl.ds(r0, HEAD_DIM), :] = yh.astype(BF16)
            xs.append(xb * jnp.broadcast_to(dte_t[h:h + 1, :], xb.shape).astype(BF16))
            cds.append(jnp.broadcast_to(etot[h:h + 1, :], (HEAD_DIM, STATE)))
        s_new = _dot(jnp.concatenate(xs, axis=0), b_ref[:, pl.ds(g * STATE, STATE)])
        st[pl.ds(g * gp, gp), :] = s_old * jnp.concatenate(cds, axis=0) + s_new

    return prepare, finish


def _scan_kernel(d_smem, xtf_ref, bf_ref, cf_ref, dtf_ref, h0f_ref, xtb_ref, bb_ref, cb_ref, dtb_ref, h0b_ref,
                 yf_ref, yb_ref, stf, stb):
    @pl.when(pl.program_id(1) == 0)
    def _():
        stf[...] = h0f_ref[...]
        stb[...] = h0b_ref[...]

    dirs = [_scan_dir(d_smem, xtf_ref, bf_ref, cf_ref, dtf_ref, yf_ref, stf, False),
            _scan_dir(d_smem, xtb_ref, bb_ref, cb_ref, dtb_ref, yb_ref, stb, True)]
    nxt = [prepare(0) for prepare, _ in dirs]
    for g in range(GROUPS):
        cur = nxt
        if g + 1 < GROUPS:
            nxt = [prepare(g + 1) for prepare, _ in dirs]
        for (_, finish), args in zip(dirs, cur):
            finish(g, *args)


def _scan(ssm_d, xt, bc, dtda, h0):
    n = bc.shape[0]
    bsz = n // SEQ
    nc = SEQ // QC
    fwd = lambda b, c: b * nc + c
    bwd = lambda b, c: b * nc + nc - 1 - c

    def specs(d, chunk):
        return [
            pl.BlockSpec((None, INNER, QC), lambda b, c: (chunk(b, c), 0, 0)),
            pl.BlockSpec((QC, GN), lambda b, c: (chunk(b, c), 0)),
            pl.BlockSpec((QC, GN), lambda b, c: (chunk(b, c), 1)),
            pl.BlockSpec((None, QC, LANES), lambda b, c: (d, chunk(b, c), 0)),
            pl.BlockSpec((None, None, INNER, STATE), lambda b, c: (d, b, 0, 0)),
        ]

    y_shape = jax.ShapeDtypeStruct((n // QC, INNER, QC), BF16)
    return pl.pallas_call(
        _scan_kernel,
        grid=(bsz, nc),
        in_specs=[pl.BlockSpec(memory_space=pltpu.SMEM)] + specs(0, fwd) + specs(1, bwd),
        out_specs=[pl.BlockSpec((None, INNER, QC), lambda b, c: (fwd(b, c), 0, 0)),
                   pl.BlockSpec((None, INNER, QC), lambda b, c: (bwd(b, c), 0, 0))],
        out_shape=[y_shape, y_shape],
        scratch_shapes=[pltpu.VMEM((INNER, STATE), F32), pltpu.VMEM((INNER, STATE), F32)],
        compiler_params=pltpu.CompilerParams(
            dimension_semantics=("arbitrary", "arbitrary"),
            vmem_limit_bytes=VMEM_LIMIT),
        name="scan",
    )(ssm_d, xt, bc, bc, dtda, h0, xt, bc, bc, dtda, h0)


NSEG_CONV = TM_CONV // GRID_W
assert NSEG_CONV == SUB
PAD_CONV = (CONV_K // 2) * NSEG_CONV
RB_CONV = 64
MM_CONV = 256


def _conv_kernel(x_ref, nw_ref, sh_ref, sc_ref, wglu_ref, wg_ref, dw_ref, db_ref, lnw_ref, lnb_ref,
                 wco_ref, bco_ref, a_ref, xs_scr, pad_scr, u_scr):
    _to_seg_major(xs_scr, x_ref, NSEG_CONV, GRID_W)
    h = _norm_mod(_load_cols(xs_scr), nw_ref[...], sh_ref[...], sc_ref[...]).astype(BF16)
    zeros = jnp.zeros((PAD_CONV, LANES), F32)
    per_mm = MM_CONV // LANES

    def glu(cb):
        val = _dot(h, wglu_ref[:, pl.ds(cb * MM_CONV, MM_CONV)])
        gate = _dot(h, wglu_ref[:, pl.ds(D + cb * MM_CONV, MM_CONV)])
        return val * _sigmoid(gate)

    for cb in range(D // MM_CONV):
        u = glu(cb)
        for j in range(per_mm):
            cc = cb * per_mm + j
            pad_scr[cc, pl.ds(0, PAD_CONV), :] = zeros
            pad_scr[cc, pl.ds(PAD_CONV, TM_CONV), :] = u[:, j * LANES:(j + 1) * LANES]
            pad_scr[cc, pl.ds(PAD_CONV + TM_CONV, PAD_CONV), :] = zeros
            for rb in range(TM_CONV // RB_CONV):
                r0 = rb * RB_CONV
                acc = jnp.broadcast_to(db_ref[cc], (RB_CONV, LANES))
                for k in range(CONV_K):
                    acc = acc + pad_scr[cc, pl.ds(r0 + k * NSEG_CONV, RB_CONV), :] * dw_ref[cc, pl.ds(k, 1), :]
                u_scr[cc, pl.ds(r0, RB_CONV), :] = acc

    v = _load_cols(u_scr)
    mu = jnp.mean(v, axis=-1, keepdims=True)
    vc = v - mu
    var = jnp.mean(vc * vc, axis=-1, keepdims=True)
    ln = vc * lax.rsqrt(var + EPS) * lnw_ref[...] + lnb_ref[...]
    act = _silu(ln).astype(BF16)
    u_conv = _dot(act, wco_ref[...]) + bco_ref[...]
    g_conv = _sigmoid(_dot(h, wg_ref[...]))
    _store_cols(u_scr, g_conv * u_conv)
    _from_seg_major(a_ref, 0, u_scr, NSEG_CONV, GRID_W)


def _conv_branch(x2, mod3, nw, w_rest, dw, db, lnw, lnb, wco, bco):
    n = x2.shape[0]
    tpb = SEQ // TM_CONV
    const = lambda i: (0, 0)
    return pl.pallas_call(
        _conv_kernel,
        grid=(n // TM_CONV,),
        in_specs=[
            pl.BlockSpec((TM_CONV, D), lambda i: (i, 0)),
            pl.BlockSpec((1, D), const),
            pl.BlockSpec((None, 1, D), lambda i: (i // tpb, 0, 0)),
            pl.BlockSpec((None, 1, D), lambda i: (i // tpb, 0, 1)),
            pl.BlockSpec((D, 2 * D), lambda i: (0, (COL_GLU - COL_Z) // (2 * D))),
            pl.BlockSpec((D, D), lambda i: (0, (COL_GATE - COL_Z) // D)),
            pl.BlockSpec((D // LANES, CONV_K, LANES), lambda i: (0, 0, 0)),
            pl.BlockSpec((D // LANES, 1, LANES), lambda i: (0, 0, 0)),
            pl.BlockSpec((1, D), const),
            pl.BlockSpec((1, D), const),
            pl.BlockSpec((D, D), const),
            pl.BlockSpec((1, D), const),
        ],
        out_specs=pl.BlockSpec((TM_CONV, D), lambda i: (i, 0)),
        out_shape=jax.ShapeDtypeStruct((n, D), BF16),
        scratch_shapes=[
            pltpu.VMEM((D // LANES, TM_CONV, LANES), F32),
            pltpu.VMEM((D // LANES, TM_CONV + 2 * PAD_CONV, LANES), F32),
            pltpu.VMEM((D // LANES, TM_CONV, LANES), F32),
        ],
        compiler_params=pltpu.CompilerParams(vmem_limit_bytes=VMEM_LIMIT),
        name="conv",
    )(x2, nw, mod3, mod3, w_rest, w_rest, dw, db, lnw, lnb, wco, bco)


def _merge_kernel(x_ref, yf_ref, yb_ref, a_ref, nw_ref, sh_ref, sc_ref, gt_ref, wz_ref, wg_ref,
                  snw_ref, wso_ref, wo_ref, o_ref):
    x = x_ref[...]
    h = _norm_mod(x, nw_ref[...], sh_ref[...], sc_ref[...]).astype(BF16)
    z = _dot(h, wz_ref[...])
    y = (yf_ref[...].astype(F32) + yb_ref[...].astype(F32)).T
    y = y * _silu(z)
    ms = jnp.mean(y * y, axis=-1, keepdims=True)
    yn = (y * lax.rsqrt(ms + EPS) * snw_ref[...]).astype(BF16)
    u_ssd = _dot(yn, wso_ref[...])
    g_ssd = _sigmoid(_dot(h, wg_ref[...]))
    m = a_ref[...].astype(F32) + g_ssd * u_ssd
    mix = _dot(m.astype(BF16), wo_ref[...])
    o_ref[...] = x + gt_ref[...] * mix


def _merge(x2, yf, yb, a, mod3, nw, w_rest, snw, wso, wo):
    n = x2.shape[0]
    tm = TM_MERGE
    assert tm == QC
    tpb = SEQ // tm
    const = lambda i: (0, 0)
    return pl.pallas_call(
        _merge_kernel,
        grid=(n // tm,),
        in_specs=[
            pl.BlockSpec((tm, D), lambda i: (i, 0)),
            pl.BlockSpec((None, INNER, QC), lambda i: (i, 0, 0)),
            pl.BlockSpec((None, INNER, QC), lambda i: (i, 0, 0)),
            pl.BlockSpec((tm, D), lambda i: (i, 0)),
            pl.BlockSpec((1, D), const),
            pl.BlockSpec((None, 1, D), lambda i: (i // tpb, 0, 0)),
            pl.BlockSpec((None, 1, D), lambda i: (i // tpb, 0, 1)),
            pl.BlockSpec((None, 1, D), lambda i: (i // tpb, 0, 2)),
            pl.BlockSpec((D, INNER), const),
            pl.BlockSpec((D, D), lambda i: (0, (COL_GATE - COL_Z) // D + 1)),
            pl.BlockSpec((1, INNER), const),
            pl.BlockSpec((INNER, D), const),
            pl.BlockSpec((D, D), const),
        ],
        out_specs=pl.BlockSpec((tm, D), lambda i: (i, 0)),
        out_shape=jax.ShapeDtypeStruct((n, D), F32),
        compiler_params=pltpu.CompilerParams(vmem_limit_bytes=VMEM_LIMIT),
        name="merge",
    )(x2, yf, yb, a, nw, mod3, mod3, mod3, w_rest, w_rest, snw, wso, wo)


def _mlp_kernel(x_ref, nw_ref, sh_ref, sc_ref, gt_ref, w1_ref, w2_ref, fnw_ref, o_ref):
    x = x_ref[...]
    h = _norm_mod(x, nw_ref[...], sh_ref[...], sc_ref[...]).astype(BF16)
    acc = jnp.zeros((TM_MLP, D), F32)
    for c in range(D_FF // FF_CHUNK):
        t = jnp.maximum(_dot(h, w1_ref[:, pl.ds(c * FF_CHUNK, FF_CHUNK)]), 0.0)
        acc = acc + _dot((t * t).astype(BF16), w2_ref[pl.ds(c * FF_CHUNK, FF_CHUNK), :])
    x2 = x + gt_ref[...] * acc
    ms = jnp.mean(x2 * x2, axis=-1, keepdims=True)
    o_ref[...] = x2 * lax.rsqrt(ms + EPS) * fnw_ref[...]


def _mlp(x1, mod3, nw, w1, w2, fnw):
    n = x1.shape[0]
    tm = TM_MLP
    tpb = SEQ // tm
    const = lambda i: (0, 0)
    return pl.pallas_call(
        _mlp_kernel,
        grid=(n // tm,),
        in_specs=[
            pl.BlockSpec((tm, D), lambda i: (i, 0)),
            pl.BlockSpec((1, D), const),
            pl.BlockSpec((None, 1, D), lambda i: (i // tpb, 0, 3)),
            pl.BlockSpec((None, 1, D), lambda i: (i // tpb, 0, 4)),
            pl.BlockSpec((None, 1, D), lambda i: (i // tpb, 0, 5)),
            pl.BlockSpec((D, D_FF), const),
            pl.BlockSpec((D_FF, D), const),
            pl.BlockSpec((1, D), const),
        ],
        out_specs=pl.BlockSpec((tm, D), lambda i: (i, 0)),
        out_shape=jax.ShapeDtypeStruct((n, D), F32),
        compiler_params=pltpu.CompilerParams(vmem_limit_bytes=VMEM_LIMIT),
        name="mlp",
    )(x1, nw, mod3, mod3, mod3, w1, w2, fnw)


def kernel(x, c, ctx, c_ctx, w_ada, b_ada, norm1_w, norm2_w, w_in, conv_dw_w, conv_dw_b, conv_ln_w,
           conv_ln_b, w_conv_out, b_conv_out, ssm_conv_w, ssm_conv_b, ssm_dt_bias, ssm_a_log, ssm_d,
           ssm_norm_w, w_ssm_out, w_o, w_mlp1, w_mlp2, final_norm_w):
    bsz, seq, _ = x.shape
    assert (bsz, seq, x.shape[2]) == (8, SEQ, D) and ctx.shape[1] == CTX and w_ada.shape[0] == 1
    n = bsz * seq
    x2 = x.reshape(n, D)
    row = lambda v: v.reshape(1, -1)

    wi = w_in[0]
    w_xbc = wi[:, :COL_DT].astype(BF16)
    wdt_f = wi[:, COL_DT:COL_DT + HEADS]
    wdt_b = wi[:, COL_DT + HEADS:COL_Z]
    wdt = jnp.concatenate([wdt_f, wdt_f, wdt_b, wdt_b], axis=1)
    bf_, bb_ = ssm_dt_bias[0, 0], ssm_dt_bias[0, 1]
    dtb = row(jnp.concatenate([bf_, bf_, bb_, bb_]))
    zh = jnp.zeros((HEADS,), F32)
    alog = row(jnp.concatenate([zh, ssm_a_log[0, 0], zh, ssm_a_log[0, 1]]))
    w_rest = wi[:, COL_Z:].astype(BF16)

    cond = jnp.concatenate([c, c_ctx[None, :], jnp.zeros((16 - bsz - 1, D), F32)], axis=0)
    mod = _ada(cond, w_ada[0], row(b_ada[0]))
    mod3 = mod.reshape(16, 1, 6 * D)

    nw1 = row(norm1_w[0])
    cw, cb = ssm_conv_w[0], row(ssm_conv_b[0])
    h0 = _ctx(ctx, mod3, nw1, w_xbc, wdt, dtb, alog, cw, cb)
    xt, bc, dtda = _ssd_in(x2, mod3, nw1, w_xbc, wdt, dtb, alog, cw, cb)
    yf, yb = _scan(ssm_d[0], xt, bc, dtda, h0)
    dw3 = conv_dw_w[0].reshape(CONV_K, D // LANES, LANES).transpose(1, 0, 2)
    db3 = conv_dw_b[0].reshape(D // LANES, 1, LANES)
    a = _conv_branch(x2, mod3, nw1, w_rest, dw3, db3, row(conv_ln_w[0]),
                     row(conv_ln_b[0]), w_conv_out[0].astype(BF16), row(b_conv_out[0]))
    x1 = _merge(x2, yf, yb, a, mod3, nw1, w_rest, row(ssm_norm_w[0]),
                w_ssm_out[0].astype(BF16), w_o[0].astype(BF16))
    out = _mlp(x1, mod3, row(norm2_w[0]), w_mlp1[0].astype(BF16), w_mlp2[0].astype(BF16),
               row(final_norm_w))
    return out.reshape(bsz, seq, D)
```

```python
import functools

import jax
import jax.numpy as jnp
from jax import lax
from jax.experimental import pallas as pl
from jax.experimental.pallas import tpu as pltpu

F32 = jnp.float32
BF16 = jnp.bfloat16

D = 1024
SEQ = 4096
CTX = 256
GRID_W = 64
CONV_K = 31
INNER = 2048
HEADS = 32
HEAD_DIM = 64
GROUPS = 8
STATE = 128
GN = GROUPS * STATE
SSM_K = 5
CHUNK = 128
XBC = INNER + 2 * GN
D_FF = 4 * D
COL_DT = XBC
COL_Z = COL_DT + 2 * HEADS
COL_GLU = COL_Z + INNER
COL_GATE = COL_GLU + 2 * D
EPS = 1e-6

LANES = 128
HALO = 16
VMEM_LIMIT = 56 * 1024 * 1024

QC = 256
TM_IN = 512
CN_IN = 256
RT_IN = 128
TM_CONV = 512
TM_MERGE = 256
TM_MLP = 512
FF_CHUNK = 1024


def _dot(a, b):
    return jnp.dot(a, b, preferred_element_type=F32)


def _dot_nt(a, b):
    return lax.dot_general(a, b, (((1,), (1,)), ((), ())), preferred_element_type=F32)


def _dot_tn(a, b):
    return lax.dot_general(a, b, (((0,), (0,)), ((), ())), preferred_element_type=F32)


LOG2E = 1.4426950408889634


def _sigmoid(v):
    return 1.0 / (1.0 + jnp.exp2(v * (-LOG2E)))


def _silu(v):
    return v * _sigmoid(v)


def _norm_mod(x, nw, shift, scale):
    ms = jnp.mean(x * x, axis=-1, keepdims=True)
    return x * lax.rsqrt(ms + EPS) * (nw * (1.0 + scale)) + shift


def _split3(v):
    hi = v.astype(BF16)
    r1 = v - hi.astype(F32)
    mid = r1.astype(BF16)
    lo = (r1 - mid.astype(F32)).astype(BF16)
    return hi, mid, lo


def _dt_lanes(h, wdt_ref, dtb_ref, alog_ref):
    w = wdt_ref[...]
    w_hi = w.astype(BF16)
    w_lo = (w - w_hi.astype(F32)).astype(BF16)
    h_hi = h.astype(BF16)
    h_lo = (h - h_hi.astype(F32)).astype(BF16)
    both = _dot(h_hi, jnp.concatenate([w_hi, w_lo], axis=1))
    raw = both[:, :LANES] + both[:, LANES:] + _dot(h_lo, w_hi)
    v = raw + dtb_ref[...]
    dt = jnp.maximum(v, 0.0) + jnp.log1p(jnp.exp(-jnp.abs(v)))
    lane = lax.broadcasted_iota(jnp.int32, (1, LANES), 1)
    mult = jnp.where((lane % 64) < HEADS, 1.0, -jnp.exp(alog_ref[...]))
    return dt * mult


def _ada_kernel(cond_ref, w_ref, b_ref, o_ref):
    c = cond_ref[...]
    s = _silu(c)
    o_ref[...] = _dot(s.astype(BF16), w_ref[...].astype(BF16)) + b_ref[...]


def _ada(cond, w, b):
    n = w.shape[1]
    tn = 1024
    return pl.pallas_call(
        _ada_kernel,
        grid=(n // tn,),
        in_specs=[
            pl.BlockSpec((16, D), lambda j: (0, 0)),
            pl.BlockSpec((D, tn), lambda j: (0, j)),
            pl.BlockSpec((1, tn), lambda j: (0, j)),
        ],
        out_specs=pl.BlockSpec((16, tn), lambda j: (0, j)),
        out_shape=jax.ShapeDtypeStruct((16, n), F32),
        name="ada",
    )(cond, w, b)


def _conv5_silu(p_scr, rows, step, cw, cb):
    acc = cb
    for k in range(SSM_K):
        acc = acc + p_scr[pl.ds(HALO - (SSM_K // 2) * step + k * step, rows), :] * cw[k:k + 1, :]
    return _silu(acc)


SUB = 8
NSEG_IN = SUB
SEG_IN = TM_IN // NSEG_IN


def _to_seg_major(dst, src, nseg, seg):
    for j in range(dst.shape[0]):
        for s in range(nseg):
            dst[j, pl.ds(s, seg, stride=nseg), :] = src[pl.ds(s * seg, seg), pl.ds(j * LANES, LANES)]


def _from_seg_major(dst, col0, src, nseg, seg):
    for j in range(src.shape[0]):
        for s in range(nseg):
            dst[pl.ds(s * seg, seg), pl.ds(col0 + j * LANES, LANES)] = (
                src[j, pl.ds(s, seg, stride=nseg), :].astype(dst.dtype))


def _load_cols(ref):
    return jnp.concatenate([ref[j] for j in range(ref.shape[0])], axis=1)


def _store_cols(ref, v):
    for j in range(ref.shape[0]):
        ref[j] = v[:, j * LANES:(j + 1) * LANES]


def _ssd_in_kernel(xp_ref, x_ref, xn_ref, nw_ref, sh_ref, sc_ref, w_ref, wdt_ref, dtb_ref,
                   alog_ref, cw_ref, cb_ref, xt_ref, bc_ref, dt_ref, xs_scr, h_scr, p_scr, r_scr, v_scr):
    i = pl.program_id(0)
    tpb = SEQ // TM_IN
    first = (i % tpb) == 0
    last = (i % tpb) == tpb - 1
    nw, sh, sc = nw_ref[...], sh_ref[...], sc_ref[...]
    _to_seg_major(xs_scr, x_ref, NSEG_IN, SEG_IN)
    hm = _norm_mod(_load_cols(xs_scr), nw, sh, sc)
    h_scr[pl.ds(0, TM_IN), :] = hm.astype(BF16)
    h_scr[pl.ds(TM_IN, HALO), :] = _norm_mod(xp_ref[...], nw, sh, sc).astype(BF16)
    h_scr[pl.ds(TM_IN + HALO, HALO), :] = _norm_mod(xn_ref[...], nw, sh, sc).astype(BF16)

    val = _dt_lanes(hm, wdt_ref, dtb_ref, alog_ref)
    lane = lax.broadcasted_iota(jnp.int32, (1, LANES), 1)
    v_scr[0] = jnp.where(lane < 64, val, 0.0)
    v_scr[1] = jnp.where(lane < 64, pltpu.roll(val, 64, axis=1), 0.0)
    for d in range(2):
        for s in range(NSEG_IN):
            dt_ref[d, pl.ds(s * SEG_IN, SEG_IN), :] = v_scr[d, pl.ds(s, SEG_IN, stride=NSEG_IN), :]

    sub = lax.broadcasted_iota(jnp.int32, (SUB, CN_IN), 0)
    keep_prev = jnp.where(first, 0.0, 1.0)
    keep_next = jnp.where(last, 0.0, 1.0)
    nblk = TM_IN // SUB
    for c in range(XBC // CN_IN):
        cols = pl.ds(c * CN_IN, CN_IN)
        pp = p_scr.at[c % 2]
        rr = r_scr.at[c % 2]
        for rt in range(TM_IN // RT_IN - 1):
            pp[pl.ds(HALO + rt * RT_IN, RT_IN), :] = _dot(h_scr[pl.ds(rt * RT_IN, RT_IN), :], w_ref[:, cols])
        last = TM_IN - RT_IN
        p = _dot(h_scr[pl.ds(last, RT_IN + 2 * HALO), :], w_ref[:, cols])
        pp[pl.ds(HALO + last, RT_IN), :] = p[0:RT_IN]
        prev = p[RT_IN + HALO - SUB:RT_IN + HALO] * keep_prev
        nxt = p[RT_IN + HALO:RT_IN + HALO + SUB] * keep_next
        blk = lambda b: pp[pl.ds(HALO + b * SUB, SUB), :]
        pp[pl.ds(HALO - SUB, SUB), :] = jnp.where(
            sub == 0, pltpu.roll(prev, 1, axis=0), pltpu.roll(blk(nblk - 1), 1, axis=0))
        pp[pl.ds(HALO - 2 * SUB, SUB), :] = jnp.where(
            sub == 0, pltpu.roll(prev, 2, axis=0), pltpu.roll(blk(nblk - 2), 1, axis=0))
        pp[pl.ds(HALO + TM_IN, SUB), :] = jnp.where(
            sub == SUB - 1, pltpu.roll(nxt, SUB - 1, axis=0), pltpu.roll(blk(0), SUB - 1, axis=0))
        pp[pl.ds(HALO + TM_IN + SUB, SUB), :] = jnp.where(
            sub == SUB - 1, pltpu.roll(nxt, SUB - 2, axis=0), pltpu.roll(blk(1), SUB - 1, axis=0))
        _store_cols(rr, _conv5_silu(pp, TM_IN, SUB, cw_ref[:, cols], cb_ref[:, cols]))
        if c * CN_IN >= INNER:
            _from_seg_major(bc_ref, c * CN_IN - INNER, rr, NSEG_IN, SEG_IN)
        else:
            for j in range(CN_IN // LANES):
                for m in range(TM_IN // LANES):
                    tok = jnp.concatenate(
                        [rr[j, pl.ds(2 * m + e, SEG_IN, stride=NSEG_IN), :] for e in range(LANES // SEG_IN)],
                        axis=0)
                    xt_ref[(m * LANES) // QC, pl.ds(c * CN_IN + j * LANES, LANES),
                           pl.ds((m * LANES) % QC, LANES)] = tok.T.astype(BF16)


def _ssd_in(x2, mod3, nw, w_xbc, wdt, dtb, alog, cw, cb):
    n = x2.shape[0]
    nt = n // TM_IN
    tpb = SEQ // TM_IN
    hb = TM_IN // HALO
    nhb = n // HALO
    const = lambda i: (0, 0)
    return pl.pallas_call(
        _ssd_in_kernel,
        grid=(nt,),
        in_specs=[
            pl.BlockSpec((HALO, D), lambda i: (jnp.maximum(i * hb - 1, 0), 0)),
            pl.BlockSpec((TM_IN, D), lambda i: (i, 0)),
            pl.BlockSpec((HALO, D), lambda i: (jnp.minimum((i + 1) * hb, nhb - 1), 0)),
            pl.BlockSpec((1, D), const),
            pl.BlockSpec((None, 1, D), lambda i: (i // tpb, 0, 0)),
            pl.BlockSpec((None, 1, D), lambda i: (i // tpb, 0, 1)),
            pl.BlockSpec((D, XBC), const),
            pl.BlockSpec((D, LANES), const),
            pl.BlockSpec((1, LANES), const),
            pl.BlockSpec((1, LANES), const),
            pl.BlockSpec((SSM_K, XBC), const),
            pl.BlockSpec((1, XBC), const),
        ],
        out_specs=[
            pl.BlockSpec((TM_IN // QC, INNER, QC), lambda i: (i, 0, 0)),
            pl.BlockSpec((TM_IN, 2 * GN), lambda i: (i, 0)),
            pl.BlockSpec((2, TM_IN, LANES), lambda i: (0, i, 0)),
        ],
        out_shape=[
            jax.ShapeDtypeStruct((n // QC, INNER, QC), BF16),
            jax.ShapeDtypeStruct((n, 2 * GN), BF16),
            jax.ShapeDtypeStruct((2, n, LANES), F32),
        ],
        scratch_shapes=[
            pltpu.VMEM((D // LANES, TM_IN, LANES), F32),
            pltpu.VMEM((TM_IN + 2 * HALO, D), BF16),
            pltpu.VMEM((2, TM_IN + 2 * HALO, CN_IN), F32),
            pltpu.VMEM((2, CN_IN // LANES, TM_IN, LANES), F32),
            pltpu.VMEM((2, TM_IN, LANES), F32),
        ],
        compiler_params=pltpu.CompilerParams(vmem_limit_bytes=VMEM_LIMIT),
        name="ssd_in",
    )(x2, x2, x2, nw, mod3, mod3, w_xbc, wdt, dtb, alog, cw, cb)


CTX_COLS = INNER + GN


def _ctx_kernel(c_ref, nw_ref, sh_ref, sc_ref, w_ref, wdt_ref, dtb_ref, alog_ref, cw_ref, cb_ref,
                h0_ref, p_scr, xb_scr):
    hc = _norm_mod(c_ref[...], nw_ref[...], sh_ref[...], sc_ref[...])
    hb = hc.astype(BF16)
    p_scr[pl.ds(0, HALO), :] = jnp.zeros((HALO, CN_IN), F32)
    p_scr[pl.ds(HALO + CTX, HALO), :] = jnp.zeros((HALO, CN_IN), F32)
    for c in range(CTX_COLS // CN_IN):
        cols = pl.ds(c * CN_IN, CN_IN)
        p_scr[pl.ds(HALO, CTX), :] = _dot(hb, w_ref[:, cols])
        xb_scr[:, cols] = _conv5_silu(p_scr, CTX, 1, cw_ref[:, cols], cb_ref[:, cols]).astype(BF16)

    val = _dt_lanes(hc, wdt_ref, dtb_ref, alog_ref)
    row = lax.broadcasted_iota(jnp.int32, (CTX, CTX), 0)
    col = lax.broadcasted_iota(jnp.int32, (CTX, CTX), 1)
    tri = (row >= col).astype(BF16)
    v_hi, v_mid, v_lo = _split3(val)
    cum = _dot(tri, v_hi) + _dot(tri, v_mid) + _dot(tri, v_lo)
    tot = cum[CTX - 1:CTX, :]
    dts = pltpu.roll(val, HEADS, axis=1)
    lane = lax.broadcasted_iota(jnp.int32, (1, LANES), 1)
    wgt = jnp.where(lane < 64, jnp.exp(tot - cum), jnp.exp(cum - val)) * dts
    lane2 = lax.broadcasted_iota(jnp.int32, (CTX, LANES), 1)
    for d in range(2):
        base = d * 64 + HEADS
        for pair in range(HEADS // 2):
            g = pair // 2
            la = base + 2 * pair
            wcol = jnp.where(lane2 < HEAD_DIM, wgt[:, la:la + 1], wgt[:, la + 1:la + 2])
            xs = (xb_scr[:, pl.ds(pair * LANES, LANES)].astype(F32) * wcol).astype(BF16)
            bg = xb_scr[:, pl.ds(INNER + g * STATE, STATE)]
            h0_ref[d, pl.ds(pair * LANES, LANES), :] = _dot_tn(xs, bg)


def _ctx(ctx, mod3, nw, w_xbc, wdt, dtb, alog, cw, cb):
    b = ctx.shape[0]
    const = lambda i: (0, 0)
    return pl.pallas_call(
        _ctx_kernel,
        grid=(b,),
        in_specs=[
            pl.BlockSpec((None, CTX, D), lambda i: (i, 0, 0)),
            pl.BlockSpec((1, D), const),
            pl.BlockSpec((None, 1, D), lambda i: (8, 0, 0)),
            pl.BlockSpec((None, 1, D), lambda i: (8, 0, 1)),
            pl.BlockSpec((D, CTX_COLS), const),
            pl.BlockSpec((D, LANES), const),
            pl.BlockSpec((1, LANES), const),
            pl.BlockSpec((1, LANES), const),
            pl.BlockSpec((SSM_K, CTX_COLS), const),
            pl.BlockSpec((1, CTX_COLS), const),
        ],
        out_specs=pl.BlockSpec((2, None, INNER, STATE), lambda i: (0, i, 0, 0)),
        out_shape=jax.ShapeDtypeStruct((2, b, INNER, STATE), F32),
        scratch_shapes=[
            pltpu.VMEM((CTX + 2 * HALO, CN_IN), F32),
            pltpu.VMEM((CTX, CTX_COLS), BF16),
        ],
        compiler_params=pltpu.CompilerParams(vmem_limit_bytes=VMEM_LIMIT),
        name="ctx",
    )(ctx, nw, mod3, mod3, w_xbc, wdt, dtb, alog, cw, cb)


def _scan_dir(d_smem, xt_ref, b_ref, c_ref, dt_ref, y_ref, st, bwd):
    q = QC
    hq = q // 2
    dtb = dt_ref[...]
    row = lax.broadcasted_iota(jnp.int32, (q, q), 0)
    col = lax.broadcasted_iota(jnp.int32, (q, q), 1)
    tri = ((col >= row) if bwd else (row >= col)).astype(BF16)
    v_hi, v_mid, v_lo = _split3(dtb)
    cum = (_dot(tri, v_hi) + _dot(tri, v_mid) + _dot(tri, v_lo)) * LOG2E
    cum_t = cum.T[HEADS:2 * HEADS, :]
    dt_t = dtb.T[0:HEADS, :]
    tot_t = cum_t[:, 0:1] if bwd else cum_t[:, q - 1:q]
    ecum_t = jnp.exp2(cum_t)
    dte_t = jnp.exp2(tot_t - cum_t) * dt_t
    etot = jnp.exp2(tot_t)
    keep = ((row >= col) if bwd else (col >= row)).astype(F32)
    fr = pl.ds(hq, hq) if bwd else pl.ds(0, hq)
    pr = pl.ds(0, hq) if bwd else pl.ds(hq, hq)
    fs = slice(hq, q) if bwd else slice(0, hq)
    ps = slice(0, hq) if bwd else slice(hq, q)
    zero = jnp.zeros((hq, hq), BF16)

    gp = (HEADS // GROUPS) * HEAD_DIM
    hpg = HEADS // GROUPS

    def prepare(g):
        cg = c_ref[:, pl.ds(g * STATE, STATE)]
        s_old = st[pl.ds(g * gp, gp), :]
        both = _dot_nt(jnp.concatenate([b_ref[fr, pl.ds(g * STATE, STATE)], s_old.astype(BF16)], axis=0), cg)
        cb_diag = (both[0:hq, fs] * keep[fs, fs]).astype(BF16)
        cb_off = both[0:hq, ps].astype(BF16)
        cbp = (_dot_nt(b_ref[pr, pl.ds(g * STATE, STATE)], c_ref[pr, pl.ds(g * STATE, STATE)])
               * keep[ps, ps]).astype(BF16)
        mts = []
        for j in range(hpg):
            h = g * hpg + j
            crow = cum_t[h:h + 1, :]
            ccol = cum[:, HEADS + h:HEADS + h + 1]
            m_diag = cb_diag * jnp.exp2(jnp.minimum(crow[:, fs] - ccol[fs], 0.0)).astype(BF16)
            m_off = cb_off * jnp.exp2(crow[:, ps] - ccol[fs]).astype(BF16)
            mp = cbp * jnp.exp2(jnp.minimum(crow[:, ps] - ccol[ps], 0.0)).astype(BF16)
            if bwd:
                mts.append(jnp.concatenate([jnp.concatenate([mp, zero], axis=1),
                                            jnp.concatenate([m_off, m_diag], axis=1)], axis=0))
            else:
                mts.append(jnp.concatenate([jnp.concatenate([m_diag, m_off], axis=1),
                                            jnp.concatenate([zero, mp], axis=1)], axis=0))
        return s_old, both[hq:hq + gp], mts

    def finish(g, s_old, y_off, mts):
        xs, cds = [], []
        for j in range(hpg):
            h = g * hpg + j
            r0 = g * gp + j * HEAD_DIM
            xb = xt_ref[pl.ds(r0, HEAD_DIM), :]
            y_diag = _dot(xb * jnp.broadcast_to(dt_t[h:h + 1, :], xb.shape).astype(BF16), mts[j])
            yh = y_diag + y_off[j * HEAD_DIM:(j + 1) * HEAD_DIM] * ecum_t[h:h + 1, :]
            if not bwd:
                yh = yh + d_smem[h] * xb.astype(F32)
            y_ref[pl.ds(r0, HEAD_DIM), :] = yh.astype(BF16)
            xs.append(xb * jnp.broadcast_to(dte_t[h:h + 1, :], xb.shape).astype(BF16))
            cds.append(jnp.broadcast_to(etot[h:h + 1, :], (HEAD_DIM, STATE)))
        s_new = _dot(jnp.concatenate(xs, axis=0), b_ref[:, pl.ds(g * STATE, STATE)])
        st[pl.ds(g * gp, gp), :] = s_old * jnp.concatenate(cds, axis=0) + s_new

    return prepare, finish


def _scan_kernel(d_smem, xtf_ref, bf_ref, cf_ref, dtf_ref, h0f_ref, xtb_ref, bb_ref, cb_ref, dtb_ref, h0b_ref,
                 yf_ref, yb_ref, stf, stb):
    @pl.when(pl.program_id(1) == 0)
    def _():
        stf[...] = h0f_ref[...]
        stb[...] = h0b_ref[...]

    dirs = [_scan_dir(d_smem, xtf_ref, bf_ref, cf_ref, dtf_ref, yf_ref, stf, False),
            _scan_dir(d_smem, xtb_ref, bb_ref, cb_ref, dtb_ref, yb_ref, stb, True)]
    nxt = [prepare(0) for prepare, _ in dirs]
    for g in range(GROUPS):
        cur = nxt
        if g + 1 < GROUPS:
            nxt = [prepare(g + 1) for prepare, _ in dirs]
        for (_, finish), args in zip(dirs, cur):
            finish(g, *args)


def _scan(ssm_d, xt, bc, dtda, h0):
    n = bc.shape[0]
    bsz = n // SEQ
    nc = SEQ // QC
    fwd = lambda b, c: b * nc + c
    bwd = lambda b, c: b * nc + nc - 1 - c

    def specs(d, chunk):
        return [
            pl.BlockSpec((None, INNER, QC), lambda b, c: (chunk(b, c), 0, 0)),
            pl.BlockSpec((QC, GN), lambda b, c: (chunk(b, c), 0)),
            pl.BlockSpec((QC, GN), lambda b, c: (chunk(b, c), 1)),
            pl.BlockSpec((None, QC, LANES), lambda b, c: (d, chunk(b, c), 0)),
            pl.BlockSpec((None, None, INNER, STATE), lambda b, c: (d, b, 0, 0)),
        ]

    y_shape = jax.ShapeDtypeStruct((n // QC, INNER, QC), BF16)
    return pl.pallas_call(
        _scan_kernel,
        grid=(bsz, nc),
        in_specs=[pl.BlockSpec(memory_space=pltpu.SMEM)] + specs(0, fwd) + specs(1, bwd),
        out_specs=[pl.BlockSpec((None, INNER, QC), lambda b, c: (fwd(b, c), 0, 0)),
                   pl.BlockSpec((None, INNER, QC), lambda b, c: (bwd(b, c), 0, 0))],
        out_shape=[y_shape, y_shape],
        scratch_shapes=[pltpu.VMEM((INNER, STATE), F32), pltpu.VMEM((INNER, STATE), F32)],
        compiler_params=pltpu.CompilerParams(
            dimension_semantics=("arbitrary", "arbitrary"),
            vmem_limit_bytes=VMEM_LIMIT),
        name="scan",
    )(ssm_d, xt, bc, bc, dtda, h0, xt, bc, bc, dtda, h0)


NSEG_CONV = TM_CONV // GRID_W
assert NSEG_CONV == SUB
RB_CONV = 64
MM_CONV = 256
TAPS_CONV = 8


def _conv_kernel(x_ref, nw_ref, sh_ref, sc_ref, wglu_ref, wg_ref, dw_ref, db_ref, lnw_ref, lnb_ref,
                 wco_ref, bco_ref, a_ref, xs_scr, pad_scr, u_scr):
    _to_seg_major(xs_scr, x_ref, NSEG_CONV, GRID_W)
    h = _norm_mod(_load_cols(xs_scr), nw_ref[...], sh_ref[...], sc_ref[...]).astype(BF16)
    per_mm = MM_CONV // LANES

    def glu(cb):
        val = _dot(h, wglu_ref[:, pl.ds(cb * MM_CONV, MM_CONV)])
        gate = _dot(h, wglu_ref[:, pl.ds(D + cb * MM_CONV, MM_CONV)])
        return val * _sigmoid(gate)

    for cb in range(D // MM_CONV):
        u = glu(cb)
        for j in range(per_mm):
            cc = cb * per_mm + j
            pad_scr[cc] = u[:, j * LANES:(j + 1) * LANES]
            nout = RB_CONV // SUB
            for rb in range(TM_CONV // RB_CONV):
                acc = [db_ref[cc]] * nout
                for k0 in range(0, CONV_K, TAPS_CONV):
                    nk = min(TAPS_CONV, CONV_K - k0)
                    base = rb * nout + k0 - CONV_K // 2
                    src = {i: pad_scr[cc, pl.ds((base + i) * SUB, SUB), :]
                           for i in range(nout + nk - 1) if 0 <= base + i < GRID_W}
                    for kk in range(nk):
                        w = dw_ref[cc, k0 + kk]
                        for o in range(nout):
                            if o + kk in src:
                                acc[o] = acc[o] + src[o + kk] * w
                for o in range(nout):
                    u_scr[cc, pl.ds((rb * nout + o) * SUB, SUB), :] = acc[o]

    v = _load_cols(u_scr)
    mu = jnp.mean(v, axis=-1, keepdims=True)
    vc = v - mu
    var = jnp.mean(vc * vc, axis=-1, keepdims=True)
    ln = vc * lax.rsqrt(var + EPS) * lnw_ref[...] + lnb_ref[...]
    act = _silu(ln).astype(BF16)
    u_conv = _dot(act, wco_ref[...]) + bco_ref[...]
    g_conv = _sigmoid(_dot(h, wg_ref[...]))
    _store_cols(u_scr, g_conv * u_conv)
    _from_seg_major(a_ref, 0, u_scr, NSEG_CONV, GRID_W)


def _conv_branch(x2, mod3, nw, w_rest, dw, db, lnw, lnb, wco, bco):
    n = x2.shape[0]
    tpb = SEQ // TM_CONV
    const = lambda i: (0, 0)
    return pl.pallas_call(
        _conv_kernel,
        grid=(n // TM_CONV,),
        in_specs=[
            pl.BlockSpec((TM_CONV, D), lambda i: (i, 0)),
            pl.BlockSpec((1, D), const),
            pl.BlockSpec((None, 1, D), lambda i: (i // tpb, 0, 0)),
            pl.BlockSpec((None, 1, D), lambda i: (i // tpb, 0, 1)),
            pl.BlockSpec((D, 2 * D), lambda i: (0, (COL_GLU - COL_Z) // (2 * D))),
            pl.BlockSpec((D, D), lambda i: (0, (COL_GATE - COL_Z) // D)),
            pl.BlockSpec((D // LANES, CONV_K, SUB, LANES), lambda i: (0, 0, 0, 0)),
            pl.BlockSpec((D // LANES, SUB, LANES), lambda i: (0, 0, 0)),
            pl.BlockSpec((1, D), const),
            pl.BlockSpec((1, D), const),
            pl.BlockSpec((D, D), const),
            pl.BlockSpec((1, D), const),
        ],
        out_specs=pl.BlockSpec((TM_CONV, D), lambda i: (i, 0)),
        out_shape=jax.ShapeDtypeStruct((n, D), BF16),
        scratch_shapes=[
            pltpu.VMEM((D // LANES, TM_CONV, LANES), F32),
            pltpu.VMEM((D // LANES, TM_CONV, LANES), F32),
            pltpu.VMEM((D // LANES, TM_CONV, LANES), F32),
        ],
        compiler_params=pltpu.CompilerParams(vmem_limit_bytes=VMEM_LIMIT),
        name="conv",
    )(x2, nw, mod3, mod3, w_rest, w_rest, dw, db, lnw, lnb, wco, bco)


def _merge_kernel(x_ref, yf_ref, yb_ref, a_ref, nw_ref, sh_ref, sc_ref, gt_ref, wz_ref, wg_ref,
                  snw_ref, wso_ref, wo_ref, o_ref):
    x = x_ref[...]
    h = _norm_mod(x, nw_ref[...], sh_ref[...], sc_ref[...]).astype(BF16)
    z = _dot(h, wz_ref[...])
    y = (yf_ref[...].astype(F32) + yb_ref[...].astype(F32)).T
    y = y * _silu(z)
    ms = jnp.mean(y * y, axis=-1, keepdims=True)
    yn = (y * lax.rsqrt(ms + EPS) * snw_ref[...]).astype(BF16)
    u_ssd = _dot(yn, wso_ref[...])
    g_ssd = _sigmoid(_dot(h, wg_ref[...]))
    m = a_ref[...].astype(F32) + g_ssd * u_ssd
    mix = _dot(m.astype(BF16), wo_ref[...])
    o_ref[...] = x + gt_ref[...] * mix


def _merge(x2, yf, yb, a, mod3, nw, w_rest, snw, wso, wo):
    n = x2.shape[0]
    tm = TM_MERGE
    assert tm == QC
    tpb = SEQ // tm
    const = lambda i: (0, 0)
    return pl.pallas_call(
        _merge_kernel,
        grid=(n // tm,),
        in_specs=[
            pl.BlockSpec((tm, D), lambda i: (i, 0)),
            pl.BlockSpec((None, INNER, QC), lambda i: (i, 0, 0)),
            pl.BlockSpec((None, INNER, QC), lambda i: (i, 0, 0)),
            pl.BlockSpec((tm, D), lambda i: (i, 0)),
            pl.BlockSpec((1, D), const),
            pl.BlockSpec((None, 1, D), lambda i: (i // tpb, 0, 0)),
            pl.BlockSpec((None, 1, D), lambda i: (i // tpb, 0, 1)),
            pl.BlockSpec((None, 1, D), lambda i: (i // tpb, 0, 2)),
            pl.BlockSpec((D, INNER), const),
            pl.BlockSpec((D, D), lambda i: (0, (COL_GATE - COL_Z) // D + 1)),
            pl.BlockSpec((1, INNER), const),
            pl.BlockSpec((INNER, D), const),
            pl.BlockSpec((D, D), const),
        ],
        out_specs=pl.BlockSpec((tm, D), lambda i: (i, 0)),
        out_shape=jax.ShapeDtypeStruct((n, D), F32),
        compiler_params=pltpu.CompilerParams(vmem_limit_bytes=VMEM_LIMIT),
        name="merge",
    )(x2, yf, yb, a, nw, mod3, mod3, mod3, w_rest, w_rest, snw, wso, wo)


def _mlp_kernel(x_ref, nw_ref, sh_ref, sc_ref, gt_ref, w1_ref, w2_ref, fnw_ref, o_ref):
    x = x_ref[...]
    h = _norm_mod(x, nw_ref[...], sh_ref[...], sc_ref[...]).astype(BF16)
    acc = jnp.zeros((TM_MLP, D), F32)
    for c in range(D_FF // FF_CHUNK):
        t = jnp.maximum(_dot(h, w1_ref[:, pl.ds(c * FF_CHUNK, FF_CHUNK)]), 0.0)
        acc = acc + _dot((t * t).astype(BF16), w2_ref[pl.ds(c * FF_CHUNK, FF_CHUNK), :])
    x2 = x + gt_ref[...] * acc
    ms = jnp.mean(x2 * x2, axis=-1, keepdims=True)
    o_ref[...] = x2 * lax.rsqrt(ms + EPS) * fnw_ref[...]


def _mlp(x1, mod3, nw, w1, w2, fnw):
    n = x1.shape[0]
    tm = TM_MLP
    tpb = SEQ // tm
    const = lambda i: (0, 0)
    return pl.pallas_call(
        _mlp_kernel,
        grid=(n // tm,),
        in_specs=[
            pl.BlockSpec((tm, D), lambda i: (i, 0)),
            pl.BlockSpec((1, D), const),
            pl.BlockSpec((None, 1, D), lambda i: (i // tpb, 0, 3)),
            pl.BlockSpec((None, 1, D), lambda i: (i // tpb, 0, 4)),
            pl.BlockSpec((None, 1, D), lambda i: (i // tpb, 0, 5)),
            pl.BlockSpec((D, D_FF), const),
            pl.BlockSpec((D_FF, D), const),
            pl.BlockSpec((1, D), const),
        ],
        out_specs=pl.BlockSpec((tm, D), lambda i: (i, 0)),
        out_shape=jax.ShapeDtypeStruct((n, D), F32),
        compiler_params=pltpu.CompilerParams(vmem_limit_bytes=VMEM_LIMIT),
        name="mlp",
    )(x1, nw, mod3, mod3, mod3, w1, w2, fnw)


def kernel(x, c, ctx, c_ctx, w_ada, b_ada, norm1_w, norm2_w, w_in, conv_dw_w, conv_dw_b, conv_ln_w,
           conv_ln_b, w_conv_out, b_conv_out, ssm_conv_w, ssm_conv_b, ssm_dt_bias, ssm_a_log, ssm_d,
           ssm_norm_w, w_ssm_out, w_o, w_mlp1, w_mlp2, final_norm_w):
    bsz, seq, _ = x.shape
    assert (bsz, seq, x.shape[2]) == (8, SEQ, D) and ctx.shape[1] == CTX and w_ada.shape[0] == 1
    n = bsz * seq
    x2 = x.reshape(n, D)
    row = lambda v: v.reshape(1, -1)

    wi = w_in[0]
    w_xbc = wi[:, :COL_DT].astype(BF16)
    wdt_f = wi[:, COL_DT:COL_DT + HEADS]
    wdt_b = wi[:, COL_DT + HEADS:COL_Z]
    wdt = jnp.concatenate([wdt_f, wdt_f, wdt_b, wdt_b], axis=1)
    bf_, bb_ = ssm_dt_bias[0, 0], ssm_dt_bias[0, 1]
    dtb = row(jnp.concatenate([bf_, bf_, bb_, bb_]))
    zh = jnp.zeros((HEADS,), F32)
    alog = row(jnp.concatenate([zh, ssm_a_log[0, 0], zh, ssm_a_log[0, 1]]))
    w_rest = wi[:, COL_Z:].astype(BF16)

    cond = jnp.concatenate([c, c_ctx[None, :], jnp.zeros((16 - bsz - 1, D), F32)], axis=0)
    mod = _ada(cond, w_ada[0], row(b_ada[0]))
    mod3 = mod.reshape(16, 1, 6 * D)

    nw1 = row(norm1_w[0])
    cw, cb = ssm_conv_w[0], row(ssm_conv_b[0])
    h0 = _ctx(ctx, mod3, nw1, w_xbc, wdt, dtb, alog, cw, cb)
    xt, bc, dtda = _ssd_in(x2, mod3, nw1, w_xbc, wdt, dtb, alog, cw, cb)
    yf, yb = _scan(ssm_d[0], xt, bc, dtda, h0)
    dw3 = jnp.broadcast_to(conv_dw_w[0].reshape(CONV_K, D // LANES, 1, LANES).transpose(1, 0, 2, 3),
                           (D // LANES, CONV_K, SUB, LANES))
    db3 = jnp.broadcast_to(conv_dw_b[0].reshape(D // LANES, 1, LANES), (D // LANES, SUB, LANES))
    a = _conv_branch(x2, mod3, nw1, w_rest, dw3, db3, row(conv_ln_w[0]),
                     row(conv_ln_b[0]), w_conv_out[0].astype(BF16), row(b_conv_out[0]))
    x1 = _merge(x2, yf, yb, a, mod3, nw1, w_rest, row(ssm_norm_w[0]),
                w_ssm_out[0].astype(BF16), w_o[0].astype(BF16))
    out = _mlp(x1, mod3, row(norm2_w[0]), w_mlp1[0].astype(BF16), w_mlp2[0].astype(BF16),
               row(final_norm_w))
    return out.reshape(bsz, seq, D)
```

```python
import functools

import jax
import jax.numpy as jnp
from jax import lax
from jax.experimental import pallas as pl
from jax.experimental.pallas import tpu as pltpu

F32 = jnp.float32
BF16 = jnp.bfloat16

D = 1024
SEQ = 4096
CTX = 256
GRID_W = 64
CONV_K = 31
INNER = 2048
HEADS = 32
HEAD_DIM = 64
GROUPS = 8
STATE = 128
GN = GROUPS * STATE
SSM_K = 5
CHUNK = 128
XBC = INNER + 2 * GN
D_FF = 4 * D
COL_DT = XBC
COL_Z = COL_DT + 2 * HEADS
COL_GLU = COL_Z + INNER
COL_GATE = COL_GLU + 2 * D
EPS = 1e-6

LANES = 128
HALO = 16
VMEM_LIMIT = 56 * 1024 * 1024

QC = 256
TM_IN = 512
CN_IN = 512
TM_CONV = 512
TM_MERGE = 512
TM_MLP = 512
FF_CHUNK = 1024


def _dot(a, b):
    return jnp.dot(a, b, preferred_element_type=F32)


def _dot_nt(a, b):
    return lax.dot_general(a, b, (((1,), (1,)), ((), ())), preferred_element_type=F32)


def _dot_tn(a, b):
    return lax.dot_general(a, b, (((0,), (0,)), ((), ())), preferred_element_type=F32)


LOG2E = 1.4426950408889634


def _sigmoid(v):
    return 1.0 / (1.0 + jnp.exp2(v * (-LOG2E)))


def _silu(v):
    return v * _sigmoid(v)


def _norm_mod(x, nw, shift, scale):
    ms = jnp.mean(x * x, axis=-1, keepdims=True)
    return x * lax.rsqrt(ms + EPS) * (nw * (1.0 + scale)) + shift


def _split3(v):
    hi = v.astype(BF16)
    r1 = v - hi.astype(F32)
    mid = r1.astype(BF16)
    lo = (r1 - mid.astype(F32)).astype(BF16)
    return hi, mid, lo


def _dt_lanes(h, wdt_ref, dtb_ref, alog_ref):
    w = wdt_ref[...]
    w_hi = w.astype(BF16)
    w_lo = (w - w_hi.astype(F32)).astype(BF16)
    h_hi = h.astype(BF16)
    h_lo = (h - h_hi.astype(F32)).astype(BF16)
    both = _dot(h_hi, jnp.concatenate([w_hi, w_lo], axis=1))
    raw = both[:, :LANES] + both[:, LANES:] + _dot(h_lo, w_hi)
    v = raw + dtb_ref[...]
    dt = jnp.maximum(v, 0.0) + jnp.log1p(jnp.exp(-jnp.abs(v)))
    lane = lax.broadcasted_iota(jnp.int32, (1, LANES), 1)
    mult = jnp.where((lane % 64) < HEADS, 1.0, -jnp.exp(alog_ref[...]))
    return dt * mult


def _ada_kernel(cond_ref, w_ref, b_ref, o_ref):
    c = cond_ref[...]
    s = _silu(c)
    o_ref[...] = _dot(s.astype(BF16), w_ref[...].astype(BF16)) + b_ref[...]


def _ada(cond, w, b):
    n = w.shape[1]
    tn = 1024
    return pl.pallas_call(
        _ada_kernel,
        grid=(n // tn,),
        in_specs=[
            pl.BlockSpec((16, D), lambda j: (0, 0)),
            pl.BlockSpec((D, tn), lambda j: (0, j)),
            pl.BlockSpec((1, tn), lambda j: (0, j)),
        ],
        out_specs=pl.BlockSpec((16, tn), lambda j: (0, j)),
        out_shape=jax.ShapeDtypeStruct((16, n), F32),
        name="ada",
    )(cond, w, b)


def _conv5_silu(p_scr, rows, step, cw, cb):
    acc = cb
    for k in range(SSM_K):
        acc = acc + p_scr[pl.ds(HALO - (SSM_K // 2) * step + k * step, rows), :] * cw[k:k + 1, :]
    return _silu(acc)


SUB = 8
NSEG_IN = SUB
SEG_IN = TM_IN // NSEG_IN


def _to_seg_major(dst, src, nseg, seg):
    for j in range(dst.shape[0]):
        for s in range(nseg):
            dst[j, pl.ds(s, seg, stride=nseg), :] = src[pl.ds(s * seg, seg), pl.ds(j * LANES, LANES)]


def _from_seg_major(dst, col0, src, nseg, seg):
    for j in range(src.shape[0]):
        for s in range(nseg):
            dst[pl.ds(s * seg, seg), pl.ds(col0 + j * LANES, LANES)] = (
                src[j, pl.ds(s, seg, stride=nseg), :].astype(dst.dtype))


def _load_cols(ref):
    return jnp.concatenate([ref[j] for j in range(ref.shape[0])], axis=1)


def _store_cols(ref, v):
    for j in range(ref.shape[0]):
        ref[j] = v[:, j * LANES:(j + 1) * LANES]


def _ssd_in_kernel(xp_ref, x_ref, xn_ref, nw_ref, sh_ref, sc_ref, w_ref, wdt_ref, dtb_ref,
                   alog_ref, cw_ref, cb_ref, xt_ref, bc_ref, dt_ref, xs_scr, h_scr, p_scr, r_scr, v_scr):
    i = pl.program_id(0)
    tpb = SEQ // TM_IN
    first = (i % tpb) == 0
    last = (i % tpb) == tpb - 1
    nw, sh, sc = nw_ref[...], sh_ref[...], sc_ref[...]
    _to_seg_major(xs_scr, x_ref, NSEG_IN, SEG_IN)
    hm = _norm_mod(_load_cols(xs_scr), nw, sh, sc)
    h_scr[pl.ds(0, TM_IN), :] = hm.astype(BF16)
    h_scr[pl.ds(TM_IN, HALO), :] = _norm_mod(xp_ref[...], nw, sh, sc).astype(BF16)
    h_scr[pl.ds(TM_IN + HALO, HALO), :] = _norm_mod(xn_ref[...], nw, sh, sc).astype(BF16)

    val = _dt_lanes(hm, wdt_ref, dtb_ref, alog_ref)
    lane = lax.broadcasted_iota(jnp.int32, (1, LANES), 1)
    v_scr[0] = jnp.where(lane < 64, val, 0.0)
    v_scr[1] = jnp.where(lane < 64, pltpu.roll(val, 64, axis=1), 0.0)
    for d in range(2):
        for s in range(NSEG_IN):
            dt_ref[d, pl.ds(s * SEG_IN, SEG_IN), :] = v_scr[d, pl.ds(s, SEG_IN, stride=NSEG_IN), :]

    sub = lax.broadcasted_iota(jnp.int32, (SUB, CN_IN), 0)
    keep_prev = jnp.where(first, 0.0, 1.0)
    keep_next = jnp.where(last, 0.0, 1.0)
    nblk = TM_IN // SUB
    for c in range(XBC // CN_IN):
        cols = pl.ds(c * CN_IN, CN_IN)
        p = _dot(h_scr[...], w_ref[:, cols])
        pp = p_scr.at[c % 2]
        rr = r_scr.at[c % 2]
        pp[pl.ds(HALO, TM_IN), :] = p[0:TM_IN]
        prev = p[TM_IN + HALO - SUB:TM_IN + HALO] * keep_prev
        nxt = p[TM_IN + HALO:TM_IN + HALO + SUB] * keep_next
        blk = lambda b: p[b * SUB:(b + 1) * SUB]
        pp[pl.ds(HALO - SUB, SUB), :] = jnp.where(
            sub == 0, pltpu.roll(prev, 1, axis=0), pltpu.roll(blk(nblk - 1), 1, axis=0))
        pp[pl.ds(HALO - 2 * SUB, SUB), :] = jnp.where(
            sub == 0, pltpu.roll(prev, 2, axis=0), pltpu.roll(blk(nblk - 2), 1, axis=0))
        pp[pl.ds(HALO + TM_IN, SUB), :] = jnp.where(
            sub == SUB - 1, pltpu.roll(nxt, SUB - 1, axis=0), pltpu.roll(blk(0), SUB - 1, axis=0))
        pp[pl.ds(HALO + TM_IN + SUB, SUB), :] = jnp.where(
            sub == SUB - 1, pltpu.roll(nxt, SUB - 2, axis=0), pltpu.roll(blk(1), SUB - 1, axis=0))
        _store_cols(rr, _conv5_silu(pp, TM_IN, SUB, cw_ref[:, cols], cb_ref[:, cols]))
        if c * CN_IN >= INNER:
            _from_seg_major(bc_ref, c * CN_IN - INNER, rr, NSEG_IN, SEG_IN)
        else:
            for j in range(CN_IN // LANES):
                for m in range(TM_IN // LANES):
                    tok = jnp.concatenate(
                        [rr[j, pl.ds(2 * m + e, SEG_IN, stride=NSEG_IN), :] for e in range(LANES // SEG_IN)],
                        axis=0)
                    xt_ref[(m * LANES) // QC, pl.ds(c * CN_IN + j * LANES, LANES),
                           pl.ds((m * LANES) % QC, LANES)] = tok.T.astype(BF16)


def _ssd_in(x2, mod3, nw, w_xbc, wdt, dtb, alog, cw, cb):
    n = x2.shape[0]
    nt = n // TM_IN
    tpb = SEQ // TM_IN
    hb = TM_IN // HALO
    nhb = n // HALO
    const = lambda i: (0, 0)
    return pl.pallas_call(
        _ssd_in_kernel,
        grid=(nt,),
        in_specs=[
            pl.BlockSpec((HALO, D), lambda i: (jnp.maximum(i * hb - 1, 0), 0)),
            pl.BlockSpec((TM_IN, D), lambda i: (i, 0)),
            pl.BlockSpec((HALO, D), lambda i: (jnp.minimum((i + 1) * hb, nhb - 1), 0)),
            pl.BlockSpec((1, D), const),
            pl.BlockSpec((None, 1, D), lambda i: (i // tpb, 0, 0)),
            pl.BlockSpec((None, 1, D), lambda i: (i // tpb, 0, 1)),
            pl.BlockSpec((D, XBC), const),
            pl.BlockSpec((D, LANES), const),
            pl.BlockSpec((1, LANES), const),
            pl.BlockSpec((1, LANES), const),
            pl.BlockSpec((SSM_K, XBC), const),
            pl.BlockSpec((1, XBC), const),
        ],
        out_specs=[
            pl.BlockSpec((TM_IN // QC, INNER, QC), lambda i: (i, 0, 0)),
            pl.BlockSpec((TM_IN, 2 * GN), lambda i: (i, 0)),
            pl.BlockSpec((2, TM_IN, LANES), lambda i: (0, i, 0)),
        ],
        out_shape=[
            jax.ShapeDtypeStruct((n // QC, INNER, QC), BF16),
            jax.ShapeDtypeStruct((n, 2 * GN), BF16),
            jax.ShapeDtypeStruct((2, n, LANES), F32),
        ],
        scratch_shapes=[
            pltpu.VMEM((D // LANES, TM_IN, LANES), F32),
            pltpu.VMEM((TM_IN + 2 * HALO, D), BF16),
            pltpu.VMEM((2, TM_IN + 2 * HALO, CN_IN), F32),
            pltpu.VMEM((2, CN_IN // LANES, TM_IN, LANES), F32),
            pltpu.VMEM((2, TM_IN, LANES), F32),
        ],
        compiler_params=pltpu.CompilerParams(vmem_limit_bytes=VMEM_LIMIT),
        name="ssd_in",
    )(x2, x2, x2, nw, mod3, mod3, w_xbc, wdt, dtb, alog, cw, cb)


CTX_COLS = INNER + GN


def _ctx_kernel(c_ref, nw_ref, sh_ref, sc_ref, w_ref, wdt_ref, dtb_ref, alog_ref, cw_ref, cb_ref,
                h0_ref, p_scr, xb_scr):
    hc = _norm_mod(c_ref[...], nw_ref[...], sh_ref[...], sc_ref[...])
    hb = hc.astype(BF16)
    p_scr[pl.ds(0, HALO), :] = jnp.zeros((HALO, CN_IN), F32)
    p_scr[pl.ds(HALO + CTX, HALO), :] = jnp.zeros((HALO, CN_IN), F32)
    for c in range(CTX_COLS // CN_IN):
        cols = pl.ds(c * CN_IN, CN_IN)
        p_scr[pl.ds(HALO, CTX), :] = _dot(hb, w_ref[:, cols])
        xb_scr[:, cols] = _conv5_silu(p_scr, CTX, 1, cw_ref[:, cols], cb_ref[:, cols]).astype(BF16)

    val = _dt_lanes(hc, wdt_ref, dtb_ref, alog_ref)
    row = lax.broadcasted_iota(jnp.int32, (CTX, CTX), 0)
    col = lax.broadcasted_iota(jnp.int32, (CTX, CTX), 1)
    tri = (row >= col).astype(BF16)
    v_hi, v_mid, v_lo = _split3(val)
    cum = _dot(tri, v_hi) + _dot(tri, v_mid) + _dot(tri, v_lo)
    tot = cum[CTX - 1:CTX, :]
    dts = pltpu.roll(val, HEADS, axis=1)
    lane = lax.broadcasted_iota(jnp.int32, (1, LANES), 1)
    wgt = jnp.where(lane < 64, jnp.exp(tot - cum), jnp.exp(cum - val)) * dts
    lane2 = lax.broadcasted_iota(jnp.int32, (CTX, LANES), 1)
    for d in range(2):
        base = d * 64 + HEADS
        for pair in range(HEADS // 2):
            g = pair // 2
            la = base + 2 * pair
            wcol = jnp.where(lane2 < HEAD_DIM, wgt[:, la:la + 1], wgt[:, la + 1:la + 2])
            xs = (xb_scr[:, pl.ds(pair * LANES, LANES)].astype(F32) * wcol).astype(BF16)
            bg = xb_scr[:, pl.ds(INNER + g * STATE, STATE)]
            h0_ref[d, pl.ds(pair * LANES, LANES), :] = _dot_tn(xs, bg)


def _ctx(ctx, mod3, nw, w_xbc, wdt, dtb, alog, cw, cb):
    b = ctx.shape[0]
    const = lambda i: (0, 0)
    return pl.pallas_call(
        _ctx_kernel,
        grid=(b,),
        in_specs=[
            pl.BlockSpec((None, CTX, D), lambda i: (i, 0, 0)),
            pl.BlockSpec((1, D), const),
            pl.BlockSpec((None, 1, D), lambda i: (8, 0, 0)),
            pl.BlockSpec((None, 1, D), lambda i: (8, 0, 1)),
            pl.BlockSpec((D, CTX_COLS), const),
            pl.BlockSpec((D, LANES), const),
            pl.BlockSpec((1, LANES), const),
            pl.BlockSpec((1, LANES), const),
            pl.BlockSpec((SSM_K, CTX_COLS), const),
            pl.BlockSpec((1, CTX_COLS), const),
        ],
        out_specs=pl.BlockSpec((2, None, INNER, STATE), lambda i: (0, i, 0, 0)),
        out_shape=jax.ShapeDtypeStruct((2, b, INNER, STATE), F32),
        scratch_shapes=[
            pltpu.VMEM((CTX + 2 * HALO, CN_IN), F32),
            pltpu.VMEM((CTX, CTX_COLS), BF16),
        ],
        compiler_params=pltpu.CompilerParams(vmem_limit_bytes=VMEM_LIMIT),
        name="ctx",
    )(ctx, nw, mod3, mod3, w_xbc, wdt, dtb, alog, cw, cb)


def _scan_dir(d_smem, xt_ref, b_ref, c_ref, dt_ref, y_ref, st, bwd):
    q = QC
    hq = q // 2
    dtb = dt_ref[...]
    row = lax.broadcasted_iota(jnp.int32, (q, q), 0)
    col = lax.broadcasted_iota(jnp.int32, (q, q), 1)
    tri = ((col >= row) if bwd else (row >= col)).astype(BF16)
    v_hi, v_mid, v_lo = _split3(dtb)
    cum = (_dot(tri, v_hi) + _dot(tri, v_mid) + _dot(tri, v_lo)) * LOG2E
    cum_t = cum.T[HEADS:2 * HEADS, :]
    dt_t = dtb.T[0:HEADS, :]
    tot_t = cum_t[:, 0:1] if bwd else cum_t[:, q - 1:q]
    ecum_t = jnp.exp2(cum_t)
    dte_t = jnp.exp2(tot_t - cum_t) * dt_t
    etot = jnp.exp2(tot_t)
    keep = ((row >= col) if bwd else (col >= row)).astype(F32)
    fr = pl.ds(hq, hq) if bwd else pl.ds(0, hq)
    pr = pl.ds(0, hq) if bwd else pl.ds(hq, hq)
    fs = slice(hq, q) if bwd else slice(0, hq)
    ps = slice(0, hq) if bwd else slice(hq, q)
    zero = jnp.zeros((hq, hq), BF16)

    gp = (HEADS // GROUPS) * HEAD_DIM
    hpg = HEADS // GROUPS

    def prepare(g):
        cg = c_ref[:, pl.ds(g * STATE, STATE)]
        s_old = st[pl.ds(g * gp, gp), :]
        both = _dot_nt(jnp.concatenate([b_ref[fr, pl.ds(g * STATE, STATE)], s_old.astype(BF16)], axis=0), cg)
        cb_diag = (both[0:hq, fs] * keep[fs, fs]).astype(BF16)
        cb_off = both[0:hq, ps].astype(BF16)
        cbp = (_dot_nt(b_ref[pr, pl.ds(g * STATE, STATE)], c_ref[pr, pl.ds(g * STATE, STATE)])
               * keep[ps, ps]).astype(BF16)
        mts = []
        for j in range(hpg):
            h = g * hpg + j
            crow = cum_t[h:h + 1, :]
            ccol = cum[:, HEADS + h:HEADS + h + 1]
            m_diag = cb_diag * jnp.exp2(jnp.minimum(crow[:, fs] - ccol[fs], 0.0)).astype(BF16)
            m_off = cb_off * jnp.exp2(crow[:, ps] - ccol[fs]).astype(BF16)
            mp = cbp * jnp.exp2(jnp.minimum(crow[:, ps] - ccol[ps], 0.0)).astype(BF16)
            if bwd:
                mts.append(jnp.concatenate([jnp.concatenate([mp, zero], axis=1),
                                            jnp.concatenate([m_off, m_diag], axis=1)], axis=0))
            else:
                mts.append(jnp.concatenate([jnp.concatenate([m_diag, m_off], axis=1),
                                            jnp.concatenate([zero, mp], axis=1)], axis=0))
        return s_old, both[hq:hq + gp], mts

    def finish(g, s_old, y_off, mts):
        xs, cds = [], []
        for j in range(hpg):
            h = g * hpg + j
            r0 = g * gp + j * HEAD_DIM
            xb = xt_ref[pl.ds(r0, HEAD_DIM), :]
            y_diag = _dot(xb * jnp.broadcast_to(dt_t[h:h + 1, :], xb.shape).astype(BF16), mts[j])
            yh = y_diag + y_off[j * HEAD_DIM:(j + 1) * HEAD_DIM] * ecum_t[h:h + 1, :]
            if not bwd:
                yh = yh + d_smem[h] * xb.astype(F32)
            y_ref[pl.ds(r0, HEAD_DIM), :] = yh.astype(BF16)
            xs.append(xb * jnp.broadcast_to(dte_t[h:h + 1, :], xb.shape).astype(BF16))
            cds.append(jnp.broadcast_to(etot[h:h + 1, :], (HEAD_DIM, STATE)))
        s_new = _dot(jnp.concatenate(xs, axis=0), b_ref[:, pl.ds(g * STATE, STATE)])
        st[pl.ds(g * gp, gp), :] = s_old * jnp.concatenate(cds, axis=0) + s_new

    return prepare, finish


def _scan_kernel(d_smem, xtf_ref, bf_ref, cf_ref, dtf_ref, h0f_ref, xtb_ref, bb_ref, cb_ref, dtb_ref, h0b_ref,
                 yf_ref, yb_ref, stf, stb):
    @pl.when(pl.program_id(1) == 0)
    def _():
        stf[...] = h0f_ref[...]
        stb[...] = h0b_ref[...]

    dirs = [_scan_dir(d_smem, xtf_ref, bf_ref, cf_ref, dtf_ref, yf_ref, stf, False),
            _scan_dir(d_smem, xtb_ref, bb_ref, cb_ref, dtb_ref, yb_ref, stb, True)]
    nxt = [prepare(0) for prepare, _ in dirs]
    for g in range(GROUPS):
        cur = nxt
        if g + 1 < GROUPS:
            nxt = [prepare(g + 1) for prepare, _ in dirs]
        for (_, finish), args in zip(dirs, cur):
            finish(g, *args)


def _scan(ssm_d, xt, bc, dtda, h0):
    n = bc.shape[0]
    bsz = n // SEQ
    nc = SEQ // QC
    fwd = lambda b, c: b * nc + c
    bwd = lambda b, c: b * nc + nc - 1 - c

    def specs(d, chunk):
        return [
            pl.BlockSpec((None, INNER, QC), lambda b, c: (chunk(b, c), 0, 0)),
            pl.BlockSpec((QC, GN), lambda b, c: (chunk(b, c), 0)),
            pl.BlockSpec((QC, GN), lambda b, c: (chunk(b, c), 1)),
            pl.BlockSpec((None, QC, LANES), lambda b, c: (d, chunk(b, c), 0)),
            pl.BlockSpec((None, None, INNER, STATE), lambda b, c: (d, b, 0, 0)),
        ]

    y_shape = jax.ShapeDtypeStruct((n // QC, INNER, QC), BF16)
    return pl.pallas_call(
        _scan_kernel,
        grid=(bsz, nc),
        in_specs=[pl.BlockSpec(memory_space=pltpu.SMEM)] + specs(0, fwd) + specs(1, bwd),
        out_specs=[pl.BlockSpec((None, INNER, QC), lambda b, c: (fwd(b, c), 0, 0)),
                   pl.BlockSpec((None, INNER, QC), lambda b, c: (bwd(b, c), 0, 0))],
        out_shape=[y_shape, y_shape],
        scratch_shapes=[pltpu.VMEM((INNER, STATE), F32), pltpu.VMEM((INNER, STATE), F32)],
        compiler_params=pltpu.CompilerParams(
            dimension_semantics=("arbitrary", "arbitrary"),
            vmem_limit_bytes=VMEM_LIMIT),
        name="scan",
    )(ssm_d, xt, bc, bc, dtda, h0, xt, bc, bc, dtda, h0)


NSEG_CONV = TM_CONV // GRID_W
assert NSEG_CONV == SUB
RB_CONV = 64
MM_CONV = 256
TAPS_CONV = 8


def _conv_kernel(x_ref, nw_ref, sh_ref, sc_ref, wglu_ref, wg_ref, dw_ref, db_ref, lnw_ref, lnb_ref,
                 wco_ref, bco_ref, a_ref, xs_scr, pad_scr, u_scr):
    _to_seg_major(xs_scr, x_ref, NSEG_CONV, GRID_W)
    h = _norm_mod(_load_cols(xs_scr), nw_ref[...], sh_ref[...], sc_ref[...]).astype(BF16)
    per_mm = MM_CONV // LANES

    def glu(cb):
        val = _dot(h, wglu_ref[:, pl.ds(cb * MM_CONV, MM_CONV)])
        gate = _dot(h, wglu_ref[:, pl.ds(D + cb * MM_CONV, MM_CONV)])
        return val * _sigmoid(gate)

    for cb in range(D // MM_CONV):
        u = glu(cb)
        for j in range(per_mm):
            cc = cb * per_mm + j
            pad_scr[cc] = u[:, j * LANES:(j + 1) * LANES]
            nout = RB_CONV // SUB
            for rb in range(TM_CONV // RB_CONV):
                acc = [db_ref[cc]] * nout
                for k0 in range(0, CONV_K, TAPS_CONV):
                    nk = min(TAPS_CONV, CONV_K - k0)
                    base = rb * nout + k0 - CONV_K // 2
                    src = {i: pad_scr[cc, pl.ds((base + i) * SUB, SUB), :]
                           for i in range(nout + nk - 1) if 0 <= base + i < GRID_W}
                    for kk in range(nk):
                        w = dw_ref[cc, k0 + kk]
                        for o in range(nout):
                            if o + kk in src:
                                acc[o] = acc[o] + src[o + kk] * w
                for o in range(nout):
                    u_scr[cc, pl.ds((rb * nout + o) * SUB, SUB), :] = acc[o]

    v = _load_cols(u_scr)
    mu = jnp.mean(v, axis=-1, keepdims=True)
    vc = v - mu
    var = jnp.mean(vc * vc, axis=-1, keepdims=True)
    ln = vc * lax.rsqrt(var + EPS) * lnw_ref[...] + lnb_ref[...]
    act = _silu(ln).astype(BF16)
    u_conv = _dot(act, wco_ref[...]) + bco_ref[...]
    g_conv = _sigmoid(_dot(h, wg_ref[...]))
    _store_cols(u_scr, g_conv * u_conv)
    _from_seg_major(a_ref, 0, u_scr, NSEG_CONV, GRID_W)


def _conv_branch(x2, mod3, nw, w_rest, dw, db, lnw, lnb, wco, bco):
    n = x2.shape[0]
    tpb = SEQ // TM_CONV
    const = lambda i: (0, 0)
    return pl.pallas_call(
        _conv_kernel,
        grid=(n // TM_CONV,),
        in_specs=[
            pl.BlockSpec((TM_CONV, D), lambda i: (i, 0)),
            pl.BlockSpec((1, D), const),
            pl.BlockSpec((None, 1, D), lambda i: (i // tpb, 0, 0)),
            pl.BlockSpec((None, 1, D), lambda i: (i // tpb, 0, 1)),
            pl.BlockSpec((D, 2 * D), lambda i: (0, (COL_GLU - COL_Z) // (2 * D))),
            pl.BlockSpec((D, D), lambda i: (0, (COL_GATE - COL_Z) // D)),
            pl.BlockSpec((D // LANES, CONV_K, SUB, LANES), lambda i: (0, 0, 0, 0)),
            pl.BlockSpec((D // LANES, SUB, LANES), lambda i: (0, 0, 0)),
            pl.BlockSpec((1, D), const),
            pl.BlockSpec((1, D), const),
            pl.BlockSpec((D, D), const),
            pl.BlockSpec((1, D), const),
        ],
        out_specs=pl.BlockSpec((TM_CONV, D), lambda i: (i, 0)),
        out_shape=jax.ShapeDtypeStruct((n, D), BF16),
        scratch_shapes=[
            pltpu.VMEM((D // LANES, TM_CONV, LANES), F32),
            pltpu.VMEM((D // LANES, TM_CONV, LANES), F32),
            pltpu.VMEM((D // LANES, TM_CONV, LANES), F32),
        ],
        compiler_params=pltpu.CompilerParams(vmem_limit_bytes=VMEM_LIMIT),
        name="conv",
    )(x2, nw, mod3, mod3, w_rest, w_rest, dw, db, lnw, lnb, wco, bco)


def _merge_kernel(x_ref, yf_ref, yb_ref, a_ref, nw_ref, sh_ref, sc_ref, gt_ref, wz_ref, wg_ref,
                  snw_ref, wso_ref, wo_ref, o_ref):
    x = x_ref[...]
    h = _norm_mod(x, nw_ref[...], sh_ref[...], sc_ref[...]).astype(BF16)
    z = _dot(h, wz_ref[...])
    y = jnp.concatenate([(yf_ref[k].astype(F32) + yb_ref[k].astype(F32)).T for k in range(TM_MERGE // QC)], axis=0)
    y = y * _silu(z)
    ms = jnp.mean(y * y, axis=-1, keepdims=True)
    yn = (y * lax.rsqrt(ms + EPS) * snw_ref[...]).astype(BF16)
    u_ssd = _dot(yn, wso_ref[...])
    g_ssd = _sigmoid(_dot(h, wg_ref[...]))
    m = a_ref[...].astype(F32) + g_ssd * u_ssd
    mix = _dot(m.astype(BF16), wo_ref[...])
    o_ref[...] = x + gt_ref[...] * mix


def _merge(x2, yf, yb, a, mod3, nw, w_rest, snw, wso, wo):
    n = x2.shape[0]
    tm = TM_MERGE
    assert tm % QC == 0
    tpb = SEQ // tm
    const = lambda i: (0, 0)
    return pl.pallas_call(
        _merge_kernel,
        grid=(n // tm,),
        in_specs=[
            pl.BlockSpec((tm, D), lambda i: (i, 0)),
            pl.BlockSpec((tm // QC, INNER, QC), lambda i: (i, 0, 0)),
            pl.BlockSpec((tm // QC, INNER, QC), lambda i: (i, 0, 0)),
            pl.BlockSpec((tm, D), lambda i: (i, 0)),
            pl.BlockSpec((1, D), const),
            pl.BlockSpec((None, 1, D), lambda i: (i // tpb, 0, 0)),
            pl.BlockSpec((None, 1, D), lambda i: (i // tpb, 0, 1)),
            pl.BlockSpec((None, 1, D), lambda i: (i // tpb, 0, 2)),
            pl.BlockSpec((D, INNER), const),
            pl.BlockSpec((D, D), lambda i: (0, (COL_GATE - COL_Z) // D + 1)),
            pl.BlockSpec((1, INNER), const),
            pl.BlockSpec((INNER, D), const),
            pl.BlockSpec((D, D), const),
        ],
        out_specs=pl.BlockSpec((tm, D), lambda i: (i, 0)),
        out_shape=jax.ShapeDtypeStruct((n, D), F32),
        compiler_params=pltpu.CompilerParams(vmem_limit_bytes=VMEM_LIMIT),
        name="merge",
    )(x2, yf, yb, a, nw, mod3, mod3, mod3, w_rest, w_rest, snw, wso, wo)


def _mlp_kernel(x_ref, nw_ref, sh_ref, sc_ref, gt_ref, w1_ref, w2_ref, fnw_ref, o_ref):
    x = x_ref[...]
    h = _norm_mod(x, nw_ref[...], sh_ref[...], sc_ref[...]).astype(BF16)
    acc = jnp.zeros((TM_MLP, D), F32)
    for c in range(D_FF // FF_CHUNK):
        t = jnp.maximum(_dot(h, w1_ref[:, pl.ds(c * FF_CHUNK, FF_CHUNK)]), 0.0)
        acc = acc + _dot((t * t).astype(BF16), w2_ref[pl.ds(c * FF_CHUNK, FF_CHUNK), :])
    x2 = x + gt_ref[...] * acc
    ms = jnp.mean(x2 * x2, axis=-1, keepdims=True)
    o_ref[...] = x2 * lax.rsqrt(ms + EPS) * fnw_ref[...]


def _mlp(x1, mod3, nw, w1, w2, fnw):
    n = x1.shape[0]
    tm = TM_MLP
    tpb = SEQ // tm
    const = lambda i: (0, 0)
    return pl.pallas_call(
        _mlp_kernel,
        grid=(n // tm,),
        in_specs=[
            pl.BlockSpec((tm, D), lambda i: (i, 0)),
            pl.BlockSpec((1, D), const),
            pl.BlockSpec((None, 1, D), lambda i: (i // tpb, 0, 3)),
            pl.BlockSpec((None, 1, D), lambda i: (i // tpb, 0, 4)),
            pl.BlockSpec((None, 1, D), lambda i: (i // tpb, 0, 5)),
            pl.BlockSpec((D, D_FF), const),
            pl.BlockSpec((D_FF, D), const),
            pl.BlockSpec((1, D), const),
        ],
        out_specs=pl.BlockSpec((tm, D), lambda i: (i, 0)),
        out_shape=jax.ShapeDtypeStruct((n, D), F32),
        compiler_params=pltpu.CompilerParams(vmem_limit_bytes=VMEM_LIMIT),
        name="mlp",
    )(x1, nw, mod3, mod3, mod3, w1, w2, fnw)


def kernel(x, c, ctx, c_ctx, w_ada, b_ada, norm1_w, norm2_w, w_in, conv_dw_w, conv_dw_b, conv_ln_w,
           conv_ln_b, w_conv_out, b_conv_out, ssm_conv_w, ssm_conv_b, ssm_dt_bias, ssm_a_log, ssm_d,
           ssm_norm_w, w_ssm_out, w_o, w_mlp1, w_mlp2, final_norm_w):
    bsz, seq, _ = x.shape
    assert (bsz, seq, x.shape[2]) == (8, SEQ, D) and ctx.shape[1] == CTX and w_ada.shape[0] == 1
    n = bsz * seq
    x2 = x.reshape(n, D)
    row = lambda v: v.reshape(1, -1)

    wi = w_in[0]
    w_xbc = wi[:, :COL_DT].astype(BF16)
    wdt_f = wi[:, COL_DT:COL_DT + HEADS]
    wdt_b = wi[:, COL_DT + HEADS:COL_Z]
    wdt = jnp.concatenate([wdt_f, wdt_f, wdt_b, wdt_b], axis=1)
    bf_, bb_ = ssm_dt_bias[0, 0], ssm_dt_bias[0, 1]
    dtb = row(jnp.concatenate([bf_, bf_, bb_, bb_]))
    zh = jnp.zeros((HEADS,), F32)
    alog = row(jnp.concatenate([zh, ssm_a_log[0, 0], zh, ssm_a_log[0, 1]]))
    w_rest = wi[:, COL_Z:].astype(BF16)

    cond = jnp.concatenate([c, c_ctx[None, :], jnp.zeros((16 - bsz - 1, D), F32)], axis=0)
    mod = _ada(cond, w_ada[0], row(b_ada[0]))
    mod3 = mod.reshape(16, 1, 6 * D)

    nw1 = row(norm1_w[0])
    cw, cb = ssm_conv_w[0], row(ssm_conv_b[0])
    h0 = _ctx(ctx, mod3, nw1, w_xbc, wdt, dtb, alog, cw, cb)
    xt, bc, dtda = _ssd_in(x2, mod3, nw1, w_xbc, wdt, dtb, alog, cw, cb)
    yf, yb = _scan(ssm_d[0], xt, bc, dtda, h0)
    dw3 = jnp.broadcast_to(conv_dw_w[0].reshape(CONV_K, D // LANES, 1, LANES).transpose(1, 0, 2, 3),
                           (D // LANES, CONV_K, SUB, LANES))
    db3 = jnp.broadcast_to(conv_dw_b[0].reshape(D // LANES, 1, LANES), (D // LANES, SUB, LANES))
    a = _conv_branch(x2, mod3, nw1, w_rest, dw3, db3, row(conv_ln_w[0]),
                     row(conv_ln_b[0]), w_conv_out[0].astype(BF16), row(b_conv_out[0]))
    x1 = _merge(x2, yf, yb, a, mod3, nw1, w_rest, row(ssm_norm_w[0]),
                w_ssm_out[0].astype(BF16), w_o[0].astype(BF16))
    out = _mlp(x1, mod3, row(norm2_w[0]), w_mlp1[0].astype(BF16), w_mlp2[0].astype(BF16),
               row(final_norm_w))
    return out.reshape(bsz, seq, D)
```

```python
import functools

import jax
import jax.numpy as jnp
from jax import lax
from jax.experimental import pallas as pl
from jax.experimental.pallas import tpu as pltpu

F32 = jnp.float32
BF16 = jnp.bfloat16

D = 1024
SEQ = 4096
CTX = 256
GRID_W = 64
CONV_K = 31
INNER = 2048
HEADS = 32
HEAD_DIM = 64
GROUPS = 8
STATE = 128
GN = GROUPS * STATE
SSM_K = 5
CHUNK = 128
XBC = INNER + 2 * GN
D_FF = 4 * D
COL_DT = XBC
COL_Z = COL_DT + 2 * HEADS
COL_GLU = COL_Z + INNER
COL_GATE = COL_GLU + 2 * D
EPS = 1e-6

LANES = 128
HALO = 16
VMEM_LIMIT = 56 * 1024 * 1024

QC = 256
TM_IN = 1024
CN_IN = 512
TM_CONV = 512
TM_MERGE = 512
TM_MLP = 1024
FF_CHUNK = 1024


def _dot(a, b):
    return jnp.dot(a, b, preferred_element_type=F32)


def _dot_nt(a, b):
    return lax.dot_general(a, b, (((1,), (1,)), ((), ())), preferred_element_type=F32)


def _dot_tn(a, b):
    return lax.dot_general(a, b, (((0,), (0,)), ((), ())), preferred_element_type=F32)


LOG2E = 1.4426950408889634


def _sigmoid(v):
    return 1.0 / (1.0 + jnp.exp2(v * (-LOG2E)))


def _silu(v):
    return v * _sigmoid(v)


def _norm_mod(x, nw, shift, scale):
    ms = jnp.mean(x * x, axis=-1, keepdims=True)
    return x * lax.rsqrt(ms + EPS) * (nw * (1.0 + scale)) + shift


def _split3(v):
    hi = v.astype(BF16)
    r1 = v - hi.astype(F32)
    mid = r1.astype(BF16)
    lo = (r1 - mid.astype(F32)).astype(BF16)
    return hi, mid, lo


def _dt_lanes(h, wdt_ref, dtb_ref, alog_ref):
    w = wdt_ref[...]
    w_hi = w.astype(BF16)
    w_lo = (w - w_hi.astype(F32)).astype(BF16)
    h_hi = h.astype(BF16)
    h_lo = (h - h_hi.astype(F32)).astype(BF16)
    both = _dot(h_hi, jnp.concatenate([w_hi, w_lo], axis=1))
    raw = both[:, :LANES] + both[:, LANES:] + _dot(h_lo, w_hi)
    v = raw + dtb_ref[...]
    dt = jnp.maximum(v, 0.0) + jnp.log1p(jnp.exp(-jnp.abs(v)))
    lane = lax.broadcasted_iota(jnp.int32, (1, LANES), 1)
    mult = jnp.where((lane % 64) < HEADS, 1.0, -jnp.exp(alog_ref[...]))
    return dt * mult


def _ada_kernel(cond_ref, w_ref, b_ref, o_ref):
    c = cond_ref[...]
    s = _silu(c)
    o_ref[...] = _dot(s.astype(BF16), w_ref[...].astype(BF16)) + b_ref[...]


def _ada(cond, w, b):
    n = w.shape[1]
    tn = 1024
    return pl.pallas_call(
        _ada_kernel,
        grid=(n // tn,),
        in_specs=[
            pl.BlockSpec((16, D), lambda j: (0, 0)),
            pl.BlockSpec((D, tn), lambda j: (0, j)),
            pl.BlockSpec((1, tn), lambda j: (0, j)),
        ],
        out_specs=pl.BlockSpec((16, tn), lambda j: (0, j)),
        out_shape=jax.ShapeDtypeStruct((16, n), F32),
        name="ada",
    )(cond, w, b)


def _conv5_silu(p_scr, rows, step, cw, cb):
    acc = cb
    for k in range(SSM_K):
        acc = acc + p_scr[pl.ds(HALO - (SSM_K // 2) * step + k * step, rows), :] * cw[k:k + 1, :]
    return _silu(acc)


SUB = 8
NSEG_IN = SUB
SEG_IN = TM_IN // NSEG_IN


def _to_seg_major(dst, src, nseg, seg):
    for j in range(dst.shape[0]):
        for s in range(nseg):
            dst[j, pl.ds(s, seg, stride=nseg), :] = src[pl.ds(s * seg, seg), pl.ds(j * LANES, LANES)]


def _from_seg_major(dst, col0, src, nseg, seg):
    for j in range(src.shape[0]):
        for s in range(nseg):
            dst[pl.ds(s * seg, seg), pl.ds(col0 + j * LANES, LANES)] = (
                src[j, pl.ds(s, seg, stride=nseg), :].astype(dst.dtype))


def _load_cols(ref):
    return jnp.concatenate([ref[j] for j in range(ref.shape[0])], axis=1)


def _store_cols(ref, v):
    for j in range(ref.shape[0]):
        ref[j] = v[:, j * LANES:(j + 1) * LANES]


def _ssd_in_kernel(xp_ref, x_ref, xn_ref, nw_ref, sh_ref, sc_ref, w_ref, wdt_ref, dtb_ref,
                   alog_ref, cw_ref, cb_ref, xt_ref, bc_ref, dt_ref, xs_scr, h_scr, p_scr, r_scr, v_scr):
    i = pl.program_id(0)
    tpb = SEQ // TM_IN
    first = (i % tpb) == 0
    last = (i % tpb) == tpb - 1
    nw, sh, sc = nw_ref[...], sh_ref[...], sc_ref[...]
    _to_seg_major(xs_scr, x_ref, NSEG_IN, SEG_IN)
    hm = _norm_mod(_load_cols(xs_scr), nw, sh, sc)
    h_scr[pl.ds(0, TM_IN), :] = hm.astype(BF16)
    h_scr[pl.ds(TM_IN, HALO), :] = _norm_mod(xp_ref[...], nw, sh, sc).astype(BF16)
    h_scr[pl.ds(TM_IN + HALO, HALO), :] = _norm_mod(xn_ref[...], nw, sh, sc).astype(BF16)

    val = _dt_lanes(hm, wdt_ref, dtb_ref, alog_ref)
    lane = lax.broadcasted_iota(jnp.int32, (1, LANES), 1)
    v_scr[0] = jnp.where(lane < 64, val, 0.0)
    v_scr[1] = jnp.where(lane < 64, pltpu.roll(val, 64, axis=1), 0.0)
    for d in range(2):
        for s in range(NSEG_IN):
            dt_ref[d, pl.ds(s * SEG_IN, SEG_IN), :] = v_scr[d, pl.ds(s, SEG_IN, stride=NSEG_IN), :]

    sub = lax.broadcasted_iota(jnp.int32, (SUB, CN_IN), 0)
    keep_prev = jnp.where(first, 0.0, 1.0)
    keep_next = jnp.where(last, 0.0, 1.0)
    nblk = TM_IN // SUB
    for c in range(XBC // CN_IN):
        cols = pl.ds(c * CN_IN, CN_IN)
        p = _dot(h_scr[...], w_ref[:, cols])
        pp = p_scr.at[c % 2]
        rr = r_scr.at[c % 2]
        pp[pl.ds(HALO, TM_IN), :] = p[0:TM_IN]
        prev = p[TM_IN + HALO - SUB:TM_IN + HALO] * keep_prev
        nxt = p[TM_IN + HALO:TM_IN + HALO + SUB] * keep_next
        blk = lambda b: p[b * SUB:(b + 1) * SUB]
        pp[pl.ds(HALO - SUB, SUB), :] = jnp.where(
            sub == 0, pltpu.roll(prev, 1, axis=0), pltpu.roll(blk(nblk - 1), 1, axis=0))
        pp[pl.ds(HALO - 2 * SUB, SUB), :] = jnp.where(
            sub == 0, pltpu.roll(prev, 2, axis=0), pltpu.roll(blk(nblk - 2), 1, axis=0))
        pp[pl.ds(HALO + TM_IN, SUB), :] = jnp.where(
            sub == SUB - 1, pltpu.roll(nxt, SUB - 1, axis=0), pltpu.roll(blk(0), SUB - 1, axis=0))
        pp[pl.ds(HALO + TM_IN + SUB, SUB), :] = jnp.where(
            sub == SUB - 1, pltpu.roll(nxt, SUB - 2, axis=0), pltpu.roll(blk(1), SUB - 1, axis=0))
        _store_cols(rr, _conv5_silu(pp, TM_IN, SUB, cw_ref[:, cols], cb_ref[:, cols]))
        if c * CN_IN >= INNER:
            _from_seg_major(bc_ref, c * CN_IN - INNER, rr, NSEG_IN, SEG_IN)
        else:
            for j in range(CN_IN // LANES):
                for m in range(TM_IN // LANES):
                    tok = jnp.concatenate(
                        [rr[j, pl.ds(m * (LANES // SEG_IN) + e, SEG_IN, stride=NSEG_IN), :]
                         for e in range(LANES // SEG_IN)],
                        axis=0)
                    xt_ref[(m * LANES) // QC, pl.ds(c * CN_IN + j * LANES, LANES),
                           pl.ds((m * LANES) % QC, LANES)] = tok.T.astype(BF16)


def _ssd_in(x2, mod3, nw, w_xbc, wdt, dtb, alog, cw, cb):
    n = x2.shape[0]
    nt = n // TM_IN
    tpb = SEQ // TM_IN
    hb = TM_IN // HALO
    nhb = n // HALO
    const = lambda i: (0, 0)
    return pl.pallas_call(
        _ssd_in_kernel,
        grid=(nt,),
        in_specs=[
            pl.BlockSpec((HALO, D), lambda i: (jnp.maximum(i * hb - 1, 0), 0)),
            pl.BlockSpec((TM_IN, D), lambda i: (i, 0)),
            pl.BlockSpec((HALO, D), lambda i: (jnp.minimum((i + 1) * hb, nhb - 1), 0)),
            pl.BlockSpec((1, D), const),
            pl.BlockSpec((None, 1, D), lambda i: (i // tpb, 0, 0)),
            pl.BlockSpec((None, 1, D), lambda i: (i // tpb, 0, 1)),
            pl.BlockSpec((D, XBC), const, pipeline_mode=pl.Buffered(1)),
            pl.BlockSpec((D, LANES), const),
            pl.BlockSpec((1, LANES), const),
            pl.BlockSpec((1, LANES), const),
            pl.BlockSpec((SSM_K, XBC), const),
            pl.BlockSpec((1, XBC), const),
        ],
        out_specs=[
            pl.BlockSpec((TM_IN // QC, INNER, QC), lambda i: (i, 0, 0)),
            pl.BlockSpec((TM_IN, 2 * GN), lambda i: (i, 0)),
            pl.BlockSpec((2, TM_IN, LANES), lambda i: (0, i, 0)),
        ],
        out_shape=[
            jax.ShapeDtypeStruct((n // QC, INNER, QC), BF16),
            jax.ShapeDtypeStruct((n, 2 * GN), BF16),
            jax.ShapeDtypeStruct((2, n, LANES), F32),
        ],
        scratch_shapes=[
            pltpu.VMEM((D // LANES, TM_IN, LANES), F32),
            pltpu.VMEM((TM_IN + 2 * HALO, D), BF16),
            pltpu.VMEM((2, TM_IN + 2 * HALO, CN_IN), F32),
            pltpu.VMEM((2, CN_IN // LANES, TM_IN, LANES), F32),
            pltpu.VMEM((2, TM_IN, LANES), F32),
        ],
        compiler_params=pltpu.CompilerParams(vmem_limit_bytes=VMEM_LIMIT),
        name="ssd_in",
    )(x2, x2, x2, nw, mod3, mod3, w_xbc, wdt, dtb, alog, cw, cb)


CTX_COLS = INNER + GN


def _ctx_kernel(c_ref, nw_ref, sh_ref, sc_ref, w_ref, wdt_ref, dtb_ref, alog_ref, cw_ref, cb_ref,
                h0_ref, p_scr, xb_scr):
    hc = _norm_mod(c_ref[...], nw_ref[...], sh_ref[...], sc_ref[...])
    hb = hc.astype(BF16)
    p_scr[pl.ds(0, HALO), :] = jnp.zeros((HALO, CN_IN), F32)
    p_scr[pl.ds(HALO + CTX, HALO), :] = jnp.zeros((HALO, CN_IN), F32)
    for c in range(CTX_COLS // CN_IN):
        cols = pl.ds(c * CN_IN, CN_IN)
        p_scr[pl.ds(HALO, CTX), :] = _dot(hb, w_ref[:, cols])
        xb_scr[:, cols] = _conv5_silu(p_scr, CTX, 1, cw_ref[:, cols], cb_ref[:, cols]).astype(BF16)

    val = _dt_lanes(hc, wdt_ref, dtb_ref, alog_ref)
    row = lax.broadcasted_iota(jnp.int32, (CTX, CTX), 0)
    col = lax.broadcasted_iota(jnp.int32, (CTX, CTX), 1)
    tri = (row >= col).astype(BF16)
    v_hi, v_mid, v_lo = _split3(val)
    cum = _dot(tri, v_hi) + _dot(tri, v_mid) + _dot(tri, v_lo)
    tot = cum[CTX - 1:CTX, :]
    dts = pltpu.roll(val, HEADS, axis=1)
    lane = lax.broadcasted_iota(jnp.int32, (1, LANES), 1)
    wgt = jnp.where(lane < 64, jnp.exp(tot - cum), jnp.exp(cum - val)) * dts
    lane2 = lax.broadcasted_iota(jnp.int32, (CTX, LANES), 1)
    for d in range(2):
        base = d * 64 + HEADS
        for pair in range(HEADS // 2):
            g = pair // 2
            la = base + 2 * pair
            wcol = jnp.where(lane2 < HEAD_DIM, wgt[:, la:la + 1], wgt[:, la + 1:la + 2])
            xs = (xb_scr[:, pl.ds(pair * LANES, LANES)].astype(F32) * wcol).astype(BF16)
            bg = xb_scr[:, pl.ds(INNER + g * STATE, STATE)]
            h0_ref[d, pl.ds(pair * LANES, LANES), :] = _dot_tn(xs, bg)


def _ctx(ctx, mod3, nw, w_xbc, wdt, dtb, alog, cw, cb):
    b = ctx.shape[0]
    const = lambda i: (0, 0)
    return pl.pallas_call(
        _ctx_kernel,
        grid=(b,),
        in_specs=[
            pl.BlockSpec((None, CTX, D), lambda i: (i, 0, 0)),
            pl.BlockSpec((1, D), const),
            pl.BlockSpec((None, 1, D), lambda i: (8, 0, 0)),
            pl.BlockSpec((None, 1, D), lambda i: (8, 0, 1)),
            pl.BlockSpec((D, CTX_COLS), const),
            pl.BlockSpec((D, LANES), const),
            pl.BlockSpec((1, LANES), const),
            pl.BlockSpec((1, LANES), const),
            pl.BlockSpec((SSM_K, CTX_COLS), const),
            pl.BlockSpec((1, CTX_COLS), const),
        ],
        out_specs=pl.BlockSpec((2, None, INNER, STATE), lambda i: (0, i, 0, 0)),
        out_shape=jax.ShapeDtypeStruct((2, b, INNER, STATE), F32),
        scratch_shapes=[
            pltpu.VMEM((CTX + 2 * HALO, CN_IN), F32),
            pltpu.VMEM((CTX, CTX_COLS), BF16),
        ],
        compiler_params=pltpu.CompilerParams(vmem_limit_bytes=VMEM_LIMIT),
        name="ctx",
    )(ctx, nw, mod3, mod3, w_xbc, wdt, dtb, alog, cw, cb)


def _scan_dir(d_smem, xt_ref, b_ref, c_ref, dt_ref, y_ref, st, bwd):
    q = QC
    hq = q // 2
    dtb = dt_ref[...]
    row = lax.broadcasted_iota(jnp.int32, (q, q), 0)
    col = lax.broadcasted_iota(jnp.int32, (q, q), 1)
    tri = ((col >= row) if bwd else (row >= col)).astype(BF16)
    v_hi, v_mid, v_lo = _split3(dtb)
    cum = (_dot(tri, v_hi) + _dot(tri, v_mid) + _dot(tri, v_lo)) * LOG2E
    cum_t = cum.T[HEADS:2 * HEADS, :]
    dt_t = dtb.T[0:HEADS, :]
    tot_t = cum_t[:, 0:1] if bwd else cum_t[:, q - 1:q]
    ecum_t = jnp.exp2(cum_t)
    dte_t = jnp.exp2(tot_t - cum_t) * dt_t
    etot = jnp.exp2(tot_t)
    keep = ((row >= col) if bwd else (col >= row)).astype(F32)
    fr = pl.ds(hq, hq) if bwd else pl.ds(0, hq)
    pr = pl.ds(0, hq) if bwd else pl.ds(hq, hq)
    fs = slice(hq, q) if bwd else slice(0, hq)
    ps = slice(0, hq) if bwd else slice(hq, q)
    zero = jnp.zeros((hq, hq), BF16)

    gp = (HEADS // GROUPS) * HEAD_DIM
    hpg = HEADS // GROUPS

    def prepare(g):
        cg = c_ref[:, pl.ds(g * STATE, STATE)]
        s_old = st[pl.ds(g * gp, gp), :]
        both = _dot_nt(jnp.concatenate([b_ref[fr, pl.ds(g * STATE, STATE)], s_old.astype(BF16)], axis=0), cg)
        cb_diag = (both[0:hq, fs] * keep[fs, fs]).astype(BF16)
        cb_off = both[0:hq, ps].astype(BF16)
        cbp = (_dot_nt(b_ref[pr, pl.ds(g * STATE, STATE)], c_ref[pr, pl.ds(g * STATE, STATE)])
               * keep[ps, ps]).astype(BF16)
        mts = []
        for j in range(hpg):
            h = g * hpg + j
            crow = cum_t[h:h + 1, :]
            ccol = cum[:, HEADS + h:HEADS + h + 1]
            m_diag = cb_diag * jnp.exp2(jnp.minimum(crow[:, fs] - ccol[fs], 0.0)).astype(BF16)
            m_off = cb_off * jnp.exp2(crow[:, ps] - ccol[fs]).astype(BF16)
            mp = cbp * jnp.exp2(jnp.minimum(crow[:, ps] - ccol[ps], 0.0)).astype(BF16)
            if bwd:
                mts.append(jnp.concatenate([jnp.concatenate([mp, zero], axis=1),
                                            jnp.concatenate([m_off, m_diag], axis=1)], axis=0))
            else:
                mts.append(jnp.concatenate([jnp.concatenate([m_diag, m_off], axis=1),
                                            jnp.concatenate([zero, mp], axis=1)], axis=0))
        return s_old, both[hq:hq + gp], mts

    def finish(g, s_old, y_off, mts):
        xs, cds = [], []
        for j in range(hpg):
            h = g * hpg + j
            r0 = g * gp + j * HEAD_DIM
            xb = xt_ref[pl.ds(r0, HEAD_DIM), :]
            y_diag = _dot(xb * jnp.broadcast_to(dt_t[h:h + 1, :], xb.shape).astype(BF16), mts[j])
            yh = y_diag + y_off[j * HEAD_DIM:(j + 1) * HEAD_DIM] * ecum_t[h:h + 1, :]
            if not bwd:
                yh = yh + d_smem[h] * xb.astype(F32)
            y_ref[pl.ds(r0, HEAD_DIM), :] = yh.astype(BF16)
            xs.append(xb * jnp.broadcast_to(dte_t[h:h + 1, :], xb.shape).astype(BF16))
            cds.append(jnp.broadcast_to(etot[h:h + 1, :], (HEAD_DIM, STATE)))
        s_new = _dot(jnp.concatenate(xs, axis=0), b_ref[:, pl.ds(g * STATE, STATE)])
        st[pl.ds(g * gp, gp), :] = s_old * jnp.concatenate(cds, axis=0) + s_new

    return prepare, finish


def _scan_kernel(d_smem, xtf_ref, bf_ref, cf_ref, dtf_ref, h0f_ref, xtb_ref, bb_ref, cb_ref, dtb_ref, h0b_ref,
                 yf_ref, yb_ref, stf, stb):
    @pl.when(pl.program_id(1) == 0)
    def _():
        stf[...] = h0f_ref[...]
        stb[...] = h0b_ref[...]

    dirs = [_scan_dir(d_smem, xtf_ref, bf_ref, cf_ref, dtf_ref, yf_ref, stf, False),
            _scan_dir(d_smem, xtb_ref, bb_ref, cb_ref, dtb_ref, yb_ref, stb, True)]
    nxt = [prepare(0) for prepare, _ in dirs]
    for g in range(GROUPS):
        cur = nxt
        if g + 1 < GROUPS:
            nxt = [prepare(g + 1) for prepare, _ in dirs]
        for (_, finish), args in zip(dirs, cur):
            finish(g, *args)


def _scan(ssm_d, xt, bc, dtda, h0):
    n = bc.shape[0]
    bsz = n // SEQ
    nc = SEQ // QC
    fwd = lambda b, c: b * nc + c
    bwd = lambda b, c: b * nc + nc - 1 - c

    def specs(d, chunk):
        return [
            pl.BlockSpec((None, INNER, QC), lambda b, c: (chunk(b, c), 0, 0)),
            pl.BlockSpec((QC, GN), lambda b, c: (chunk(b, c), 0)),
            pl.BlockSpec((QC, GN), lambda b, c: (chunk(b, c), 1)),
            pl.BlockSpec((None, QC, LANES), lambda b, c: (d, chunk(b, c), 0)),
            pl.BlockSpec((None, None, INNER, STATE), lambda b, c: (d, b, 0, 0)),
        ]

    y_shape = jax.ShapeDtypeStruct((n // QC, INNER, QC), BF16)
    return pl.pallas_call(
        _scan_kernel,
        grid=(bsz, nc),
        in_specs=[pl.BlockSpec(memory_space=pltpu.SMEM)] + specs(0, fwd) + specs(1, bwd),
        out_specs=[pl.BlockSpec((None, INNER, QC), lambda b, c: (fwd(b, c), 0, 0)),
                   pl.BlockSpec((None, INNER, QC), lambda b, c: (bwd(b, c), 0, 0))],
        out_shape=[y_shape, y_shape],
        scratch_shapes=[pltpu.VMEM((INNER, STATE), F32), pltpu.VMEM((INNER, STATE), F32)],
        compiler_params=pltpu.CompilerParams(
            dimension_semantics=("arbitrary", "arbitrary"),
            vmem_limit_bytes=VMEM_LIMIT),
        name="scan",
    )(ssm_d, xt, bc, bc, dtda, h0, xt, bc, bc, dtda, h0)


NSEG_CONV = TM_CONV // GRID_W
assert NSEG_CONV == SUB
RB_CONV = 64
MM_CONV = 256
TAPS_CONV = 8


def _conv_kernel(x_ref, nw_ref, sh_ref, sc_ref, wglu_ref, wg_ref, dw_ref, db_ref, lnw_ref, lnb_ref,
                 wco_ref, bco_ref, a_ref, xs_scr, pad_scr, u_scr):
    _to_seg_major(xs_scr, x_ref, NSEG_CONV, GRID_W)
    h = _norm_mod(_load_cols(xs_scr), nw_ref[...], sh_ref[...], sc_ref[...]).astype(BF16)
    per_mm = MM_CONV // LANES

    def glu(cb):
        val = _dot(h, wglu_ref[:, pl.ds(cb * MM_CONV, MM_CONV)])
        gate = _dot(h, wglu_ref[:, pl.ds(D + cb * MM_CONV, MM_CONV)])
        return val * _sigmoid(gate)

    for cb in range(D // MM_CONV):
        u = glu(cb)
        for j in range(per_mm):
            cc = cb * per_mm + j
            pad_scr[cc] = u[:, j * LANES:(j + 1) * LANES]
            nout = RB_CONV // SUB
            for rb in range(TM_CONV // RB_CONV):
                acc = [db_ref[cc]] * nout
                for k0 in range(0, CONV_K, TAPS_CONV):
                    nk = min(TAPS_CONV, CONV_K - k0)
                    base = rb * nout + k0 - CONV_K // 2
                    src = {i: pad_scr[cc, pl.ds((base + i) * SUB, SUB), :]
                           for i in range(nout + nk - 1) if 0 <= base + i < GRID_W}
                    for kk in range(nk):
                        w = dw_ref[cc, k0 + kk]
                        for o in range(nout):
                            if o + kk in src:
                                acc[o] = acc[o] + src[o + kk] * w
                for o in range(nout):
                    u_scr[cc, pl.ds((rb * nout + o) * SUB, SUB), :] = acc[o]

    v = _load_cols(u_scr)
    mu = jnp.mean(v, axis=-1, keepdims=True)
    vc = v - mu
    var = jnp.mean(vc * vc, axis=-1, keepdims=True)
    ln = vc * lax.rsqrt(var + EPS) * lnw_ref[...] + lnb_ref[...]
    act = _silu(ln).astype(BF16)
    u_conv = _dot(act, wco_ref[...]) + bco_ref[...]
    g_conv = _sigmoid(_dot(h, wg_ref[...]))
    _store_cols(u_scr, g_conv * u_conv)
    _from_seg_major(a_ref, 0, u_scr, NSEG_CONV, GRID_W)


def _conv_branch(x2, mod3, nw, w_rest, dw, db, lnw, lnb, wco, bco):
    n = x2.shape[0]
    tpb = SEQ // TM_CONV
    const = lambda i: (0, 0)
    return pl.pallas_call(
        _conv_kernel,
        grid=(n // TM_CONV,),
        in_specs=[
            pl.BlockSpec((TM_CONV, D), lambda i: (i, 0)),
            pl.BlockSpec((1, D), const),
            pl.BlockSpec((None, 1, D), lambda i: (i // tpb, 0, 0)),
            pl.BlockSpec((None, 1, D), lambda i: (i // tpb, 0, 1)),
            pl.BlockSpec((D, 2 * D), lambda i: (0, (COL_GLU - COL_Z) // (2 * D))),
            pl.BlockSpec((D, D), lambda i: (0, (COL_GATE - COL_Z) // D)),
            pl.BlockSpec((D // LANES, CONV_K, SUB, LANES), lambda i: (0, 0, 0, 0)),
            pl.BlockSpec((D // LANES, SUB, LANES), lambda i: (0, 0, 0)),
            pl.BlockSpec((1, D), const),
            pl.BlockSpec((1, D), const),
            pl.BlockSpec((D, D), const),
            pl.BlockSpec((1, D), const),
        ],
        out_specs=pl.BlockSpec((TM_CONV, D), lambda i: (i, 0)),
        out_shape=jax.ShapeDtypeStruct((n, D), BF16),
        scratch_shapes=[
            pltpu.VMEM((D // LANES, TM_CONV, LANES), F32),
            pltpu.VMEM((D // LANES, TM_CONV, LANES), F32),
            pltpu.VMEM((D // LANES, TM_CONV, LANES), F32),
        ],
        compiler_params=pltpu.CompilerParams(vmem_limit_bytes=VMEM_LIMIT),
        name="conv",
    )(x2, nw, mod3, mod3, w_rest, w_rest, dw, db, lnw, lnb, wco, bco)


def _merge_kernel(x_ref, yf_ref, yb_ref, a_ref, nw_ref, sh_ref, sc_ref, gt_ref, wz_ref, wg_ref,
                  snw_ref, wso_ref, wo_ref, o_ref):
    x = x_ref[...]
    h = _norm_mod(x, nw_ref[...], sh_ref[...], sc_ref[...]).astype(BF16)
    z = _dot(h, wz_ref[...])
    y = jnp.concatenate([(yf_ref[k].astype(F32) + yb_ref[k].astype(F32)).T for k in range(TM_MERGE // QC)], axis=0)
    y = y * _silu(z)
    ms = jnp.mean(y * y, axis=-1, keepdims=True)
    yn = (y * lax.rsqrt(ms + EPS) * snw_ref[...]).astype(BF16)
    u_ssd = _dot(yn, wso_ref[...])
    g_ssd = _sigmoid(_dot(h, wg_ref[...]))
    m = a_ref[...].astype(F32) + g_ssd * u_ssd
    mix = _dot(m.astype(BF16), wo_ref[...])
    o_ref[...] = x + gt_ref[...] * mix


def _merge(x2, yf, yb, a, mod3, nw, w_rest, snw, wso, wo):
    n = x2.shape[0]
    tm = TM_MERGE
    assert tm % QC == 0
    tpb = SEQ // tm
    const = lambda i: (0, 0)
    return pl.pallas_call(
        _merge_kernel,
        grid=(n // tm,),
        in_specs=[
            pl.BlockSpec((tm, D), lambda i: (i, 0)),
            pl.BlockSpec((tm // QC, INNER, QC), lambda i: (i, 0, 0)),
            pl.BlockSpec((tm // QC, INNER, QC), lambda i: (i, 0, 0)),
            pl.BlockSpec((tm, D), lambda i: (i, 0)),
            pl.BlockSpec((1, D), const),
            pl.BlockSpec((None, 1, D), lambda i: (i // tpb, 0, 0)),
            pl.BlockSpec((None, 1, D), lambda i: (i // tpb, 0, 1)),
            pl.BlockSpec((None, 1, D), lambda i: (i // tpb, 0, 2)),
            pl.BlockSpec((D, INNER), const),
            pl.BlockSpec((D, D), lambda i: (0, (COL_GATE - COL_Z) // D + 1)),
            pl.BlockSpec((1, INNER), const),
            pl.BlockSpec((INNER, D), const),
            pl.BlockSpec((D, D), const),
        ],
        out_specs=pl.BlockSpec((tm, D), lambda i: (i, 0)),
        out_shape=jax.ShapeDtypeStruct((n, D), F32),
        compiler_params=pltpu.CompilerParams(vmem_limit_bytes=VMEM_LIMIT),
        name="merge",
    )(x2, yf, yb, a, nw, mod3, mod3, mod3, w_rest, w_rest, snw, wso, wo)


def _mlp_kernel(x_ref, nw_ref, sh_ref, sc_ref, gt_ref, w1_ref, w2_ref, fnw_ref, o_ref):
    x = x_ref[...]
    h = _norm_mod(x, nw_ref[...], sh_ref[...], sc_ref[...]).astype(BF16)
    acc = jnp.zeros((TM_MLP, D), F32)
    for c in range(D_FF // FF_CHUNK):
        t = jnp.maximum(_dot(h, w1_ref[:, pl.ds(c * FF_CHUNK, FF_CHUNK)]), 0.0)
        acc = acc + _dot((t * t).astype(BF16), w2_ref[pl.ds(c * FF_CHUNK, FF_CHUNK), :])
    x2 = x + gt_ref[...] * acc
    ms = jnp.mean(x2 * x2, axis=-1, keepdims=True)
    o_ref[...] = x2 * lax.rsqrt(ms + EPS) * fnw_ref[...]


def _mlp(x1, mod3, nw, w1, w2, fnw):
    n = x1.shape[0]
    tm = TM_MLP
    tpb = SEQ // tm
    const = lambda i: (0, 0)
    return pl.pallas_call(
        _mlp_kernel,
        grid=(n // tm,),
        in_specs=[
            pl.BlockSpec((tm, D), lambda i: (i, 0)),
            pl.BlockSpec((1, D), const),
            pl.BlockSpec((None, 1, D), lambda i: (i // tpb, 0, 3)),
            pl.BlockSpec((None, 1, D), lambda i: (i // tpb, 0, 4)),
            pl.BlockSpec((None, 1, D), lambda i: (i // tpb, 0, 5)),
            pl.BlockSpec((D, D_FF), const, pipeline_mode=pl.Buffered(1)),
            pl.BlockSpec((D_FF, D), const, pipeline_mode=pl.Buffered(1)),
            pl.BlockSpec((1, D), const),
        ],
        out_specs=pl.BlockSpec((tm, D), lambda i: (i, 0)),
        out_shape=jax.ShapeDtypeStruct((n, D), F32),
        compiler_params=pltpu.CompilerParams(vmem_limit_bytes=VMEM_LIMIT),
        name="mlp",
    )(x1, nw, mod3, mod3, mod3, w1, w2, fnw)


def kernel(x, c, ctx, c_ctx, w_ada, b_ada, norm1_w, norm2_w, w_in, conv_dw_w, conv_dw_b, conv_ln_w,
           conv_ln_b, w_conv_out, b_conv_out, ssm_conv_w, ssm_conv_b, ssm_dt_bias, ssm_a_log, ssm_d,
           ssm_norm_w, w_ssm_out, w_o, w_mlp1, w_mlp2, final_norm_w):
    bsz, seq, _ = x.shape
    assert (bsz, seq, x.shape[2]) == (8, SEQ, D) and ctx.shape[1] == CTX and w_ada.shape[0] == 1
    n = bsz * seq
    x2 = x.reshape(n, D)
    row = lambda v: v.reshape(1, -1)

    wi = w_in[0]
    w_xbc = wi[:, :COL_DT].astype(BF16)
    wdt_f = wi[:, COL_DT:COL_DT + HEADS]
    wdt_b = wi[:, COL_DT + HEADS:COL_Z]
    wdt = jnp.concatenate([wdt_f, wdt_f, wdt_b, wdt_b], axis=1)
    bf_, bb_ = ssm_dt_bias[0, 0], ssm_dt_bias[0, 1]
    dtb = row(jnp.concatenate([bf_, bf_, bb_, bb_]))
    zh = jnp.zeros((HEADS,), F32)
    alog = row(jnp.concatenate([zh, ssm_a_log[0, 0], zh, ssm_a_log[0, 1]]))
    w_rest = wi[:, COL_Z:].astype(BF16)

    cond = jnp.concatenate([c, c_ctx[None, :], jnp.zeros((16 - bsz - 1, D), F32)], axis=0)
    mod = _ada(cond, w_ada[0], row(b_ada[0]))
    mod3 = mod.reshape(16, 1, 6 * D)

    nw1 = row(norm1_w[0])
    cw, cb = ssm_conv_w[0], row(ssm_conv_b[0])
    h0 = _ctx(ctx, mod3, nw1, w_xbc, wdt, dtb, alog, cw, cb)
    xt, bc, dtda = _ssd_in(x2, mod3, nw1, w_xbc, wdt, dtb, alog, cw, cb)
    yf, yb = _scan(ssm_d[0], xt, bc, dtda, h0)
    dw3 = jnp.broadcast_to(conv_dw_w[0].reshape(CONV_K, D // LANES, 1, LANES).transpose(1, 0, 2, 3),
                           (D // LANES, CONV_K, SUB, LANES))
    db3 = jnp.broadcast_to(conv_dw_b[0].reshape(D // LANES, 1, LANES), (D // LANES, SUB, LANES))
    a = _conv_branch(x2, mod3, nw1, w_rest, dw3, db3, row(conv_ln_w[0]),
                     row(conv_ln_b[0]), w_conv_out[0].astype(BF16), row(b_conv_out[0]))
    x1 = _merge(x2, yf, yb, a, mod3, nw1, w_rest, row(ssm_norm_w[0]),
                w_ssm_out[0].astype(BF16), w_o[0].astype(BF16))
    out = _mlp(x1, mod3, row(norm2_w[0]), w_mlp1[0].astype(BF16), w_mlp2[0].astype(BF16),
               row(final_norm_w))
    return out.reshape(bsz, seq, D)
```

```python
import functools

import jax
import jax.numpy as jnp
from jax import lax
from jax.experimental import pallas as pl
from jax.experimental.pallas import tpu as pltpu

F32 = jnp.float32
BF16 = jnp.bfloat16

D = 1024
SEQ = 4096
CTX = 256
GRID_W = 64
CONV_K = 31
INNER = 2048
HEADS = 32
HEAD_DIM = 64
GROUPS = 8
STATE = 128
GN = GROUPS * STATE
SSM_K = 5
CHUNK = 128
XBC = INNER + 2 * GN
D_FF = 4 * D
COL_DT = XBC
COL_Z = COL_DT + 2 * HEADS
COL_GLU = COL_Z + INNER
COL_GATE = COL_GLU + 2 * D
EPS = 1e-6

LANES = 128
HALO = 16
VMEM_LIMIT = 56 * 1024 * 1024

QC = 256
TM_IN = 1024
CN_IN = 512
TM_CONV = 1024
TM_MERGE = 512
TM_MLP = 1024
FF_CHUNK = 1024


def _dot(a, b):
    return jnp.dot(a, b, preferred_element_type=F32)


def _dot_nt(a, b):
    return lax.dot_general(a, b, (((1,), (1,)), ((), ())), preferred_element_type=F32)


def _dot_tn(a, b):
    return lax.dot_general(a, b, (((0,), (0,)), ((), ())), preferred_element_type=F32)


LOG2E = 1.4426950408889634


def _sigmoid(v):
    return 1.0 / (1.0 + jnp.exp2(v * (-LOG2E)))


def _silu(v):
    return v * _sigmoid(v)


def _norm_mod(x, nw, shift, scale):
    ms = jnp.mean(x * x, axis=-1, keepdims=True)
    return x * lax.rsqrt(ms + EPS) * (nw * (1.0 + scale)) + shift


def _split3(v):
    hi = v.astype(BF16)
    r1 = v - hi.astype(F32)
    mid = r1.astype(BF16)
    lo = (r1 - mid.astype(F32)).astype(BF16)
    return hi, mid, lo


def _dt_lanes(h, wdt_ref, dtb_ref, alog_ref):
    w = wdt_ref[...]
    w_hi = w.astype(BF16)
    w_lo = (w - w_hi.astype(F32)).astype(BF16)
    h_hi = h.astype(BF16)
    h_lo = (h - h_hi.astype(F32)).astype(BF16)
    both = _dot(h_hi, jnp.concatenate([w_hi, w_lo], axis=1))
    raw = both[:, :LANES] + both[:, LANES:] + _dot(h_lo, w_hi)
    v = raw + dtb_ref[...]
    dt = jnp.maximum(v, 0.0) + jnp.log1p(jnp.exp(-jnp.abs(v)))
    lane = lax.broadcasted_iota(jnp.int32, (1, LANES), 1)
    mult = jnp.where((lane % 64) < HEADS, 1.0, -jnp.exp(alog_ref[...]))
    return dt * mult


def _ada_kernel(cond_ref, w_ref, b_ref, o_ref):
    c = cond_ref[...]
    s = _silu(c)
    o_ref[...] = _dot(s.astype(BF16), w_ref[...].astype(BF16)) + b_ref[...]


def _ada(cond, w, b):
    n = w.shape[1]
    tn = 1024
    return pl.pallas_call(
        _ada_kernel,
        grid=(n // tn,),
        in_specs=[
            pl.BlockSpec((16, D), lambda j: (0, 0)),
            pl.BlockSpec((D, tn), lambda j: (0, j)),
            pl.BlockSpec((1, tn), lambda j: (0, j)),
        ],
        out_specs=pl.BlockSpec((16, tn), lambda j: (0, j)),
        out_shape=jax.ShapeDtypeStruct((16, n), F32),
        name="ada",
    )(cond, w, b)


def _conv5_silu(p_scr, rows, step, cw, cb):
    acc = cb
    for k in range(SSM_K):
        acc = acc + p_scr[pl.ds(HALO - (SSM_K // 2) * step + k * step, rows), :] * cw[k:k + 1, :]
    return _silu(acc)


SUB = 8
NSEG_IN = SUB
SEG_IN = TM_IN // NSEG_IN


def _to_seg_major(dst, src, nseg, seg, row0=0):
    for j in range(dst.shape[0]):
        for s in range(nseg):
            dst[j, pl.ds(row0 + s, seg, stride=nseg), :] = src[pl.ds(row0 + s * seg, seg), pl.ds(j * LANES, LANES)]


def _from_seg_major(dst, col0, src, nseg, seg, row0=0):
    for j in range(src.shape[0]):
        for s in range(nseg):
            dst[pl.ds(row0 + s * seg, seg), pl.ds(col0 + j * LANES, LANES)] = (
                src[j, pl.ds(row0 + s, seg, stride=nseg), :].astype(dst.dtype))


def _load_cols(ref):
    return jnp.concatenate([ref[j] for j in range(ref.shape[0])], axis=1)


def _store_cols(ref, v):
    for j in range(ref.shape[0]):
        ref[j] = v[:, j * LANES:(j + 1) * LANES]


def _ssd_in_kernel(xp_ref, x_ref, xn_ref, nw_ref, sh_ref, sc_ref, w_ref, wdt_ref, dtb_ref,
                   alog_ref, cw_ref, cb_ref, xt_ref, bc_ref, dt_ref, xs_scr, h_scr, p_scr, r_scr, v_scr):
    i = pl.program_id(0)
    tpb = SEQ // TM_IN
    first = (i % tpb) == 0
    last = (i % tpb) == tpb - 1
    nw, sh, sc = nw_ref[...], sh_ref[...], sc_ref[...]
    _to_seg_major(xs_scr, x_ref, NSEG_IN, SEG_IN)
    hm = _norm_mod(_load_cols(xs_scr), nw, sh, sc)
    h_scr[pl.ds(0, TM_IN), :] = hm.astype(BF16)
    h_scr[pl.ds(TM_IN, HALO), :] = _norm_mod(xp_ref[...], nw, sh, sc).astype(BF16)
    h_scr[pl.ds(TM_IN + HALO, HALO), :] = _norm_mod(xn_ref[...], nw, sh, sc).astype(BF16)

    val = _dt_lanes(hm, wdt_ref, dtb_ref, alog_ref)
    lane = lax.broadcasted_iota(jnp.int32, (1, LANES), 1)
    v_scr[0] = jnp.where(lane < 64, val, 0.0)
    v_scr[1] = jnp.where(lane < 64, pltpu.roll(val, 64, axis=1), 0.0)
    for d in range(2):
        for s in range(NSEG_IN):
            dt_ref[d, pl.ds(s * SEG_IN, SEG_IN), :] = v_scr[d, pl.ds(s, SEG_IN, stride=NSEG_IN), :]

    sub = lax.broadcasted_iota(jnp.int32, (SUB, CN_IN), 0)
    keep_prev = jnp.where(first, 0.0, 1.0)
    keep_next = jnp.where(last, 0.0, 1.0)
    nblk = TM_IN // SUB
    for c in range(XBC // CN_IN):
        cols = pl.ds(c * CN_IN, CN_IN)
        p = _dot(h_scr[...], w_ref[:, cols])
        pp = p_scr.at[c % 2]
        rr = r_scr.at[c % 2]
        pp[pl.ds(HALO, TM_IN), :] = p[0:TM_IN]
        prev = p[TM_IN + HALO - SUB:TM_IN + HALO] * keep_prev
        nxt = p[TM_IN + HALO:TM_IN + HALO + SUB] * keep_next
        blk = lambda b: p[b * SUB:(b + 1) * SUB]
        pp[pl.ds(HALO - SUB, SUB), :] = jnp.where(
            sub == 0, pltpu.roll(prev, 1, axis=0), pltpu.roll(blk(nblk - 1), 1, axis=0))
        pp[pl.ds(HALO - 2 * SUB, SUB), :] = jnp.where(
            sub == 0, pltpu.roll(prev, 2, axis=0), pltpu.roll(blk(nblk - 2), 1, axis=0))
        pp[pl.ds(HALO + TM_IN, SUB), :] = jnp.where(
            sub == SUB - 1, pltpu.roll(nxt, SUB - 1, axis=0), pltpu.roll(blk(0), SUB - 1, axis=0))
        pp[pl.ds(HALO + TM_IN + SUB, SUB), :] = jnp.where(
            sub == SUB - 1, pltpu.roll(nxt, SUB - 2, axis=0), pltpu.roll(blk(1), SUB - 1, axis=0))
        _store_cols(rr, _conv5_silu(pp, TM_IN, SUB, cw_ref[:, cols], cb_ref[:, cols]))
        if c * CN_IN >= INNER:
            _from_seg_major(bc_ref, c * CN_IN - INNER, rr, NSEG_IN, SEG_IN)
        else:
            for j in range(CN_IN // LANES):
                for m in range(TM_IN // LANES):
                    tok = jnp.concatenate(
                        [rr[j, pl.ds(m * (LANES // SEG_IN) + e, SEG_IN, stride=NSEG_IN), :]
                         for e in range(LANES // SEG_IN)],
                        axis=0)
                    xt_ref[(m * LANES) // QC, pl.ds(c * CN_IN + j * LANES, LANES),
                           pl.ds((m * LANES) % QC, LANES)] = tok.T.astype(BF16)


def _ssd_in(x2, mod3, nw, w_xbc, wdt, dtb, alog, cw, cb):
    n = x2.shape[0]
    nt = n // TM_IN
    tpb = SEQ // TM_IN
    hb = TM_IN // HALO
    nhb = n // HALO
    const = lambda i: (0, 0)
    return pl.pallas_call(
        _ssd_in_kernel,
        grid=(nt,),
        in_specs=[
            pl.BlockSpec((HALO, D), lambda i: (jnp.maximum(i * hb - 1, 0), 0)),
            pl.BlockSpec((TM_IN, D), lambda i: (i, 0)),
            pl.BlockSpec((HALO, D), lambda i: (jnp.minimum((i + 1) * hb, nhb - 1), 0)),
            pl.BlockSpec((1, D), const),
            pl.BlockSpec((None, 1, D), lambda i: (i // tpb, 0, 0)),
            pl.BlockSpec((None, 1, D), lambda i: (i // tpb, 0, 1)),
            pl.BlockSpec((D, XBC), const, pipeline_mode=pl.Buffered(1)),
            pl.BlockSpec((D, LANES), const),
            pl.BlockSpec((1, LANES), const),
            pl.BlockSpec((1, LANES), const),
            pl.BlockSpec((SSM_K, XBC), const),
            pl.BlockSpec((1, XBC), const),
        ],
        out_specs=[
            pl.BlockSpec((TM_IN // QC, INNER, QC), lambda i: (i, 0, 0)),
            pl.BlockSpec((TM_IN, 2 * GN), lambda i: (i, 0)),
            pl.BlockSpec((2, TM_IN, LANES), lambda i: (0, i, 0)),
        ],
        out_shape=[
            jax.ShapeDtypeStruct((n // QC, INNER, QC), BF16),
            jax.ShapeDtypeStruct((n, 2 * GN), BF16),
            jax.ShapeDtypeStruct((2, n, LANES), F32),
        ],
        scratch_shapes=[
            pltpu.VMEM((D // LANES, TM_IN, LANES), F32),
            pltpu.VMEM((TM_IN + 2 * HALO, D), BF16),
            pltpu.VMEM((2, TM_IN + 2 * HALO, CN_IN), F32),
            pltpu.VMEM((2, CN_IN // LANES, TM_IN, LANES), F32),
            pltpu.VMEM((2, TM_IN, LANES), F32),
        ],
        compiler_params=pltpu.CompilerParams(vmem_limit_bytes=VMEM_LIMIT),
        name="ssd_in",
    )(x2, x2, x2, nw, mod3, mod3, w_xbc, wdt, dtb, alog, cw, cb)


CTX_COLS = INNER + GN


def _ctx_kernel(c_ref, nw_ref, sh_ref, sc_ref, w_ref, wdt_ref, dtb_ref, alog_ref, cw_ref, cb_ref,
                h0_ref, p_scr, xb_scr):
    hc = _norm_mod(c_ref[...], nw_ref[...], sh_ref[...], sc_ref[...])
    hb = hc.astype(BF16)
    p_scr[pl.ds(0, HALO), :] = jnp.zeros((HALO, CN_IN), F32)
    p_scr[pl.ds(HALO + CTX, HALO), :] = jnp.zeros((HALO, CN_IN), F32)
    for c in range(CTX_COLS // CN_IN):
        cols = pl.ds(c * CN_IN, CN_IN)
        p_scr[pl.ds(HALO, CTX), :] = _dot(hb, w_ref[:, cols])
        xb_scr[:, cols] = _conv5_silu(p_scr, CTX, 1, cw_ref[:, cols], cb_ref[:, cols]).astype(BF16)

    val = _dt_lanes(hc, wdt_ref, dtb_ref, alog_ref)
    row = lax.broadcasted_iota(jnp.int32, (CTX, CTX), 0)
    col = lax.broadcasted_iota(jnp.int32, (CTX, CTX), 1)
    tri = (row >= col).astype(BF16)
    v_hi, v_mid, v_lo = _split3(val)
    cum = _dot(tri, v_hi) + _dot(tri, v_mid) + _dot(tri, v_lo)
    tot = cum[CTX - 1:CTX, :]
    dts = pltpu.roll(val, HEADS, axis=1)
    lane = lax.broadcasted_iota(jnp.int32, (1, LANES), 1)
    wgt = jnp.where(lane < 64, jnp.exp(tot - cum), jnp.exp(cum - val)) * dts
    lane2 = lax.broadcasted_iota(jnp.int32, (CTX, LANES), 1)
    for d in range(2):
        base = d * 64 + HEADS
        for pair in range(HEADS // 2):
            g = pair // 2
            la = base + 2 * pair
            wcol = jnp.where(lane2 < HEAD_DIM, wgt[:, la:la + 1], wgt[:, la + 1:la + 2])
            xs = (xb_scr[:, pl.ds(pair * LANES, LANES)].astype(F32) * wcol).astype(BF16)
            bg = xb_scr[:, pl.ds(INNER + g * STATE, STATE)]
            h0_ref[d, pl.ds(pair * LANES, LANES), :] = _dot_tn(xs, bg)


def _ctx(ctx, mod3, nw, w_xbc, wdt, dtb, alog, cw, cb):
    b = ctx.shape[0]
    const = lambda i: (0, 0)
    return pl.pallas_call(
        _ctx_kernel,
        grid=(b,),
        in_specs=[
            pl.BlockSpec((None, CTX, D), lambda i: (i, 0, 0)),
            pl.BlockSpec((1, D), const),
            pl.BlockSpec((None, 1, D), lambda i: (8, 0, 0)),
            pl.BlockSpec((None, 1, D), lambda i: (8, 0, 1)),
            pl.BlockSpec((D, CTX_COLS), const),
            pl.BlockSpec((D, LANES), const),
            pl.BlockSpec((1, LANES), const),
            pl.BlockSpec((1, LANES), const),
            pl.BlockSpec((SSM_K, CTX_COLS), const),
            pl.BlockSpec((1, CTX_COLS), const),
        ],
        out_specs=pl.BlockSpec((2, None, INNER, STATE), lambda i: (0, i, 0, 0)),
        out_shape=jax.ShapeDtypeStruct((2, b, INNER, STATE), F32),
        scratch_shapes=[
            pltpu.VMEM((CTX + 2 * HALO, CN_IN), F32),
            pltpu.VMEM((CTX, CTX_COLS), BF16),
        ],
        compiler_params=pltpu.CompilerParams(vmem_limit_bytes=VMEM_LIMIT),
        name="ctx",
    )(ctx, nw, mod3, mod3, w_xbc, wdt, dtb, alog, cw, cb)


def _scan_dir(d_smem, xt_ref, b_ref, c_ref, dt_ref, y_ref, st, bwd):
    q = QC
    hq = q // 2
    dtb = dt_ref[...]
    row = lax.broadcasted_iota(jnp.int32, (q, q), 0)
    col = lax.broadcasted_iota(jnp.int32, (q, q), 1)
    tri = ((col >= row) if bwd else (row >= col)).astype(BF16)
    v_hi, v_mid, v_lo = _split3(dtb)
    cum = (_dot(tri, v_hi) + _dot(tri, v_mid) + _dot(tri, v_lo)) * LOG2E
    cum_t = cum.T[HEADS:2 * HEADS, :]
    dt_t = dtb.T[0:HEADS, :]
    tot_t = cum_t[:, 0:1] if bwd else cum_t[:, q - 1:q]
    ecum_t = jnp.exp2(cum_t)
    dte_t = jnp.exp2(tot_t - cum_t) * dt_t
    etot = jnp.exp2(tot_t)
    keep = ((row >= col) if bwd else (col >= row)).astype(F32)
    fr = pl.ds(hq, hq) if bwd else pl.ds(0, hq)
    pr = pl.ds(0, hq) if bwd else pl.ds(hq, hq)
    fs = slice(hq, q) if bwd else slice(0, hq)
    ps = slice(0, hq) if bwd else slice(hq, q)
    zero = jnp.zeros((hq, hq), BF16)

    gp = (HEADS // GROUPS) * HEAD_DIM
    hpg = HEADS // GROUPS

    def prepare(g):
        cg = c_ref[:, pl.ds(g * STATE, STATE)]
        s_old = st[pl.ds(g * gp, gp), :]
        both = _dot_nt(jnp.concatenate([b_ref[fr, pl.ds(g * STATE, STATE)], s_old.astype(BF16)], axis=0), cg)
        cb_diag = (both[0:hq, fs] * keep[fs, fs]).astype(BF16)
        cb_off = both[0:hq, ps].astype(BF16)
        cbp = (_dot_nt(b_ref[pr, pl.ds(g * STATE, STATE)], c_ref[pr, pl.ds(g * STATE, STATE)])
               * keep[ps, ps]).astype(BF16)
        mts = []
        for j in range(hpg):
            h = g * hpg + j
            crow = cum_t[h:h + 1, :]
            ccol = cum[:, HEADS + h:HEADS + h + 1]
            m_diag = cb_diag * jnp.exp2(jnp.minimum(crow[:, fs] - ccol[fs], 0.0)).astype(BF16)
            m_off = cb_off * jnp.exp2(crow[:, ps] - ccol[fs]).astype(BF16)
            mp = cbp * jnp.exp2(jnp.minimum(crow[:, ps] - ccol[ps], 0.0)).astype(BF16)
            if bwd:
                mts.append(jnp.concatenate([jnp.concatenate([mp, zero], axis=1),
                                            jnp.concatenate([m_off, m_diag], axis=1)], axis=0))
            else:
                mts.append(jnp.concatenate([jnp.concatenate([m_diag, m_off], axis=1),
                                            jnp.concatenate([zero, mp], axis=1)], axis=0))
        return s_old, both[hq:hq + gp], mts

    def finish(g, s_old, y_off, mts):
        xs, cds = [], []
        for j in range(hpg):
            h = g * hpg + j
            r0 = g * gp + j * HEAD_DIM
            xb = xt_ref[pl.ds(r0, HEAD_DIM), :]
            y_diag = _dot(xb * jnp.broadcast_to(dt_t[h:h + 1, :], xb.shape).astype(BF16), mts[j])
            yh = y_diag + y_off[j * HEAD_DIM:(j + 1) * HEAD_DIM] * ecum_t[h:h + 1, :]
            if not bwd:
                yh = yh + d_smem[h] * xb.astype(F32)
            y_ref[pl.ds(r0, HEAD_DIM), :] = yh.astype(BF16)
            xs.append(xb * jnp.broadcast_to(dte_t[h:h + 1, :], xb.shape).astype(BF16))
            cds.append(jnp.broadcast_to(etot[h:h + 1, :], (HEAD_DIM, STATE)))
        s_new = _dot(jnp.concatenate(xs, axis=0), b_ref[:, pl.ds(g * STATE, STATE)])
        st[pl.ds(g * gp, gp), :] = s_old * jnp.concatenate(cds, axis=0) + s_new

    return prepare, finish


def _scan_kernel(d_smem, xtf_ref, bf_ref, cf_ref, dtf_ref, h0f_ref, xtb_ref, bb_ref, cb_ref, dtb_ref, h0b_ref,
                 yf_ref, yb_ref, stf, stb):
    @pl.when(pl.program_id(1) == 0)
    def _():
        stf[...] = h0f_ref[...]
        stb[...] = h0b_ref[...]

    dirs = [_scan_dir(d_smem, xtf_ref, bf_ref, cf_ref, dtf_ref, yf_ref, stf, False),
            _scan_dir(d_smem, xtb_ref, bb_ref, cb_ref, dtb_ref, yb_ref, stb, True)]
    nxt = [prepare(0) for prepare, _ in dirs]
    for g in range(GROUPS):
        cur = nxt
        if g + 1 < GROUPS:
            nxt = [prepare(g + 1) for prepare, _ in dirs]
        for (_, finish), args in zip(dirs, cur):
            finish(g, *args)


def _scan(ssm_d, xt, bc, dtda, h0):
    n = bc.shape[0]
    bsz = n // SEQ
    nc = SEQ // QC
    fwd = lambda b, c: b * nc + c
    bwd = lambda b, c: b * nc + nc - 1 - c

    def specs(d, chunk):
        return [
            pl.BlockSpec((None, INNER, QC), lambda b, c: (chunk(b, c), 0, 0)),
            pl.BlockSpec((QC, GN), lambda b, c: (chunk(b, c), 0)),
            pl.BlockSpec((QC, GN), lambda b, c: (chunk(b, c), 1)),
            pl.BlockSpec((None, QC, LANES), lambda b, c: (d, chunk(b, c), 0)),
            pl.BlockSpec((None, None, INNER, STATE), lambda b, c: (d, b, 0, 0)),
        ]

    y_shape = jax.ShapeDtypeStruct((n // QC, INNER, QC), BF16)
    return pl.pallas_call(
        _scan_kernel,
        grid=(bsz, nc),
        in_specs=[pl.BlockSpec(memory_space=pltpu.SMEM)] + specs(0, fwd) + specs(1, bwd),
        out_specs=[pl.BlockSpec((None, INNER, QC), lambda b, c: (fwd(b, c), 0, 0)),
                   pl.BlockSpec((None, INNER, QC), lambda b, c: (bwd(b, c), 0, 0))],
        out_shape=[y_shape, y_shape],
        scratch_shapes=[pltpu.VMEM((INNER, STATE), F32), pltpu.VMEM((INNER, STATE), F32)],
        compiler_params=pltpu.CompilerParams(
            dimension_semantics=("arbitrary", "arbitrary"),
            vmem_limit_bytes=VMEM_LIMIT),
        name="scan",
    )(ssm_d, xt, bc, bc, dtda, h0, xt, bc, bc, dtda, h0)


NSEG_CONV = SUB
HALF_CONV = NSEG_CONV * GRID_W
assert TM_CONV % HALF_CONV == 0
RB_CONV = 64
MM_CONV = 256
TAPS_CONV = 8


def _conv_kernel(x_ref, nw_ref, sh_ref, sc_ref, wglu_ref, wg_ref, dw_ref, db_ref, lnw_ref, lnb_ref,
                 wco_ref, bco_ref, a_ref, xs_scr, pad_scr, u_scr):
    halves = [hf * HALF_CONV for hf in range(TM_CONV // HALF_CONV)]
    for r_h in halves:
        _to_seg_major(xs_scr, x_ref, NSEG_CONV, GRID_W, r_h)
    h = _norm_mod(_load_cols(xs_scr), nw_ref[...], sh_ref[...], sc_ref[...]).astype(BF16)
    per_mm = MM_CONV // LANES

    def glu(cb):
        val = _dot(h, wglu_ref[:, pl.ds(cb * MM_CONV, MM_CONV)])
        gate = _dot(h, wglu_ref[:, pl.ds(D + cb * MM_CONV, MM_CONV)])
        return val * _sigmoid(gate)

    for cb in range(D // MM_CONV):
        u = glu(cb)
        for j in range(per_mm):
            cc = cb * per_mm + j
            pad_scr[cc] = u[:, j * LANES:(j + 1) * LANES]
            nout = RB_CONV // SUB
            for r_h, rb in [(r_h, rb) for r_h in halves for rb in range(HALF_CONV // RB_CONV)]:
                acc = [db_ref[cc]] * nout
                for k0 in range(0, CONV_K, TAPS_CONV):
                    nk = min(TAPS_CONV, CONV_K - k0)
                    base = rb * nout + k0 - CONV_K // 2
                    src = {i: pad_scr[cc, pl.ds(r_h + (base + i) * SUB, SUB), :]
                           for i in range(nout + nk - 1) if 0 <= base + i < GRID_W}
                    for kk in range(nk):
                        w = dw_ref[cc, k0 + kk]
                        for o in range(nout):
                            if o + kk in src:
                                acc[o] = acc[o] + src[o + kk] * w
                for o in range(nout):
                    u_scr[cc, pl.ds(r_h + (rb * nout + o) * SUB, SUB), :] = acc[o]

    v = _load_cols(u_scr)
    mu = jnp.mean(v, axis=-1, keepdims=True)
    vc = v - mu
    var = jnp.mean(vc * vc, axis=-1, keepdims=True)
    ln = vc * lax.rsqrt(var + EPS) * lnw_ref[...] + lnb_ref[...]
    act = _silu(ln).astype(BF16)
    u_conv = _dot(act, wco_ref[...]) + bco_ref[...]
    g_conv = _sigmoid(_dot(h, wg_ref[...]))
    _store_cols(u_scr, g_conv * u_conv)
    for r_h in halves:
        _from_seg_major(a_ref, 0, u_scr, NSEG_CONV, GRID_W, r_h)


def _conv_branch(x2, mod3, nw, w_rest, dw, db, lnw, lnb, wco, bco):
    n = x2.shape[0]
    tpb = SEQ // TM_CONV
    const = lambda i: (0, 0)
    return pl.pallas_call(
        _conv_kernel,
        grid=(n // TM_CONV,),
        in_specs=[
            pl.BlockSpec((TM_CONV, D), lambda i: (i, 0)),
            pl.BlockSpec((1, D), const),
            pl.BlockSpec((None, 1, D), lambda i: (i // tpb, 0, 0)),
            pl.BlockSpec((None, 1, D), lambda i: (i // tpb, 0, 1)),
            pl.BlockSpec((D, 2 * D), lambda i: (0, (COL_GLU - COL_Z) // (2 * D))),
            pl.BlockSpec((D, D), lambda i: (0, (COL_GATE - COL_Z) // D)),
            pl.BlockSpec((D // LANES, CONV_K, SUB, LANES), lambda i: (0, 0, 0, 0)),
            pl.BlockSpec((D // LANES, SUB, LANES), lambda i: (0, 0, 0)),
            pl.BlockSpec((1, D), const),
            pl.BlockSpec((1, D), const),
            pl.BlockSpec((D, D), const),
            pl.BlockSpec((1, D), const),
        ],
        out_specs=pl.BlockSpec((TM_CONV, D), lambda i: (i, 0)),
        out_shape=jax.ShapeDtypeStruct((n, D), BF16),
        scratch_shapes=[
            pltpu.VMEM((D // LANES, TM_CONV, LANES), F32),
            pltpu.VMEM((D // LANES, TM_CONV, LANES), F32),
            pltpu.VMEM((D // LANES, TM_CONV, LANES), F32),
        ],
        compiler_params=pltpu.CompilerParams(vmem_limit_bytes=VMEM_LIMIT),
        name="conv",
    )(x2, nw, mod3, mod3, w_rest, w_rest, dw, db, lnw, lnb, wco, bco)


def _merge_kernel(x_ref, yf_ref, yb_ref, a_ref, nw_ref, sh_ref, sc_ref, gt_ref, wz_ref, wg_ref,
                  snw_ref, wso_ref, wo_ref, o_ref):
    x = x_ref[...]
    h = _norm_mod(x, nw_ref[...], sh_ref[...], sc_ref[...]).astype(BF16)
    z = _dot(h, wz_ref[...])
    y = jnp.concatenate([(yf_ref[k].astype(F32) + yb_ref[k].astype(F32)).T for k in range(TM_MERGE // QC)], axis=0)
    y = y * _silu(z)
    ms = jnp.mean(y * y, axis=-1, keepdims=True)
    yn = (y * lax.rsqrt(ms + EPS) * snw_ref[...]).astype(BF16)
    u_ssd = _dot(yn, wso_ref[...])
    g_ssd = _sigmoid(_dot(h, wg_ref[...]))
    m = a_ref[...].astype(F32) + g_ssd * u_ssd
    mix = _dot(m.astype(BF16), wo_ref[...])
    o_ref[...] = x + gt_ref[...] * mix


def _merge(x2, yf, yb, a, mod3, nw, w_rest, snw, wso, wo):
    n = x2.shape[0]
    tm = TM_MERGE
    assert tm % QC == 0
    tpb = SEQ // tm
    const = lambda i: (0, 0)
    return pl.pallas_call(
        _merge_kernel,
        grid=(n // tm,),
        in_specs=[
            pl.BlockSpec((tm, D), lambda i: (i, 0)),
            pl.BlockSpec((tm // QC, INNER, QC), lambda i: (i, 0, 0)),
            pl.BlockSpec((tm // QC, INNER, QC), lambda i: (i, 0, 0)),
            pl.BlockSpec((tm, D), lambda i: (i, 0)),
            pl.BlockSpec((1, D), const),
            pl.BlockSpec((None, 1, D), lambda i: (i // tpb, 0, 0)),
            pl.BlockSpec((None, 1, D), lambda i: (i // tpb, 0, 1)),
            pl.BlockSpec((None, 1, D), lambda i: (i // tpb, 0, 2)),
            pl.BlockSpec((D, INNER), const),
            pl.BlockSpec((D, D), lambda i: (0, (COL_GATE - COL_Z) // D + 1)),
            pl.BlockSpec((1, INNER), const),
            pl.BlockSpec((INNER, D), const),
            pl.BlockSpec((D, D), const),
        ],
        out_specs=pl.BlockSpec((tm, D), lambda i: (i, 0)),
        out_shape=jax.ShapeDtypeStruct((n, D), F32),
        compiler_params=pltpu.CompilerParams(vmem_limit_bytes=VMEM_LIMIT),
        name="merge",
    )(x2, yf, yb, a, nw, mod3, mod3, mod3, w_rest, w_rest, snw, wso, wo)


def _mlp_kernel(x_ref, nw_ref, sh_ref, sc_ref, gt_ref, w1_ref, w2_ref, fnw_ref, o_ref):
    x = x_ref[...]
    h = _norm_mod(x, nw_ref[...], sh_ref[...], sc_ref[...]).astype(BF16)
    acc = jnp.zeros((TM_MLP, D), F32)
    for c in range(D_FF // FF_CHUNK):
        t = jnp.maximum(_dot(h, w1_ref[:, pl.ds(c * FF_CHUNK, FF_CHUNK)]), 0.0)
        acc = acc + _dot((t * t).astype(BF16), w2_ref[pl.ds(c * FF_CHUNK, FF_CHUNK), :])
    x2 = x + gt_ref[...] * acc
    ms = jnp.mean(x2 * x2, axis=-1, keepdims=True)
    o_ref[...] = x2 * lax.rsqrt(ms + EPS) * fnw_ref[...]


def _mlp(x1, mod3, nw, w1, w2, fnw):
    n = x1.shape[0]
    tm = TM_MLP
    tpb = SEQ // tm
    const = lambda i: (0, 0)
    return pl.pallas_call(
        _mlp_kernel,
        grid=(n // tm,),
        in_specs=[
            pl.BlockSpec((tm, D), lambda i: (i, 0)),
            pl.BlockSpec((1, D), const),
            pl.BlockSpec((None, 1, D), lambda i: (i // tpb, 0, 3)),
            pl.BlockSpec((None, 1, D), lambda i: (i // tpb, 0, 4)),
            pl.BlockSpec((None, 1, D), lambda i: (i // tpb, 0, 5)),
            pl.BlockSpec((D, D_FF), const, pipeline_mode=pl.Buffered(1)),
            pl.BlockSpec((D_FF, D), const, pipeline_mode=pl.Buffered(1)),
            pl.BlockSpec((1, D), const),
        ],
        out_specs=pl.BlockSpec((tm, D), lambda i: (i, 0)),
        out_shape=jax.ShapeDtypeStruct((n, D), F32),
        compiler_params=pltpu.CompilerParams(vmem_limit_bytes=VMEM_LIMIT),
        name="mlp",
    )(x1, nw, mod3, mod3, mod3, w1, w2, fnw)


def kernel(x, c, ctx, c_ctx, w_ada, b_ada, norm1_w, norm2_w, w_in, conv_dw_w, conv_dw_b, conv_ln_w,
           conv_ln_b, w_conv_out, b_conv_out, ssm_conv_w, ssm_conv_b, ssm_dt_bias, ssm_a_log, ssm_d,
           ssm_norm_w, w_ssm_out, w_o, w_mlp1, w_mlp2, final_norm_w):
    bsz, seq, _ = x.shape
    assert (bsz, seq, x.shape[2]) == (8, SEQ, D) and ctx.shape[1] == CTX and w_ada.shape[0] == 1
    n = bsz * seq
    x2 = x.reshape(n, D)
    row = lambda v: v.reshape(1, -1)

    wi = w_in[0]
    w_xbc = wi[:, :COL_DT].astype(BF16)
    wdt_f = wi[:, COL_DT:COL_DT + HEADS]
    wdt_b = wi[:, COL_DT + HEADS:COL_Z]
    wdt = jnp.concatenate([wdt_f, wdt_f, wdt_b, wdt_b], axis=1)
    bf_, bb_ = ssm_dt_bias[0, 0], ssm_dt_bias[0, 1]
    dtb = row(jnp.concatenate([bf_, bf_, bb_, bb_]))
    zh = jnp.zeros((HEADS,), F32)
    alog = row(jnp.concatenate([zh, ssm_a_log[0, 0], zh, ssm_a_log[0, 1]]))
    w_rest = wi[:, COL_Z:].astype(BF16)

    cond = jnp.concatenate([c, c_ctx[None, :], jnp.zeros((16 - bsz - 1, D), F32)], axis=0)
    mod = _ada(cond, w_ada[0], row(b_ada[0]))
    mod3 = mod.reshape(16, 1, 6 * D)

    nw1 = row(norm1_w[0])
    cw, cb = ssm_conv_w[0], row(ssm_conv_b[0])
    h0 = _ctx(ctx, mod3, nw1, w_xbc, wdt, dtb, alog, cw, cb)
    xt, bc, dtda = _ssd_in(x2, mod3, nw1, w_xbc, wdt, dtb, alog, cw, cb)
    yf, yb = _scan(ssm_d[0], xt, bc, dtda, h0)
    dw3 = jnp.broadcast_to(conv_dw_w[0].reshape(CONV_K, D // LANES, 1, LANES).transpose(1, 0, 2, 3),
                           (D // LANES, CONV_K, SUB, LANES))
    db3 = jnp.broadcast_to(conv_dw_b[0].reshape(D // LANES, 1, LANES), (D // LANES, SUB, LANES))
    a = _conv_branch(x2, mod3, nw1, w_rest, dw3, db3, row(conv_ln_w[0]),
                     row(conv_ln_b[0]), w_conv_out[0].astype(BF16), row(b_conv_out[0]))
    x1 = _merge(x2, yf, yb, a, mod3, nw1, w_rest, row(ssm_norm_w[0]),
                w_ssm_out[0].astype(BF16), w_o[0].astype(BF16))
    out = _mlp(x1, mod3, row(norm2_w[0]), w_mlp1[0].astype(BF16), w_mlp2[0].astype(BF16),
               row(final_norm_w))
    return out.reshape(bsz, seq, D)
```

```python
import jax
import jax.numpy as jnp
from jax import lax
from jax.experimental import pallas as pl
from jax.experimental.pallas import tpu as pltpu

F32 = jnp.float32
BF16 = jnp.bfloat16

D = 1024
SEQ = 4096
CTX = 256
GRID_W = 64
CONV_K = 31
INNER = 2048
HEADS = 32
HEAD_DIM = 64
GROUPS = 8
STATE = 128
GN = GROUPS * STATE
SSM_K = 5
XBC = INNER + 2 * GN
D_FF = 4 * D
COL_DT = XBC
COL_Z = COL_DT + 2 * HEADS
COL_GLU = COL_Z + INNER
COL_GATE = COL_GLU + 2 * D
EPS = 1e-6

LANES = 128
HALO = 16
VMEM_LIMIT = 56 * 1024 * 1024

QC = 256
TM_IN = 1024
CN_IN = 512
TM_CONV = 1024
TM_MERGE = 512
TM_MLP = 1024
FF_CHUNK = 1024


def _dot(a, b):
    return jnp.dot(a, b, preferred_element_type=F32)


def _dot_nt(a, b):
    return lax.dot_general(a, b, (((1,), (1,)), ((), ())), preferred_element_type=F32)


def _dot_tn(a, b):
    return lax.dot_general(a, b, (((0,), (0,)), ((), ())), preferred_element_type=F32)


LOG2E = 1.4426950408889634


def _sigmoid(v):
    return 1.0 / (1.0 + jnp.exp2(v * (-LOG2E)))


def _silu(v):
    return v * _sigmoid(v)


def _norm_mod(x, nw, shift, scale):
    ms = jnp.mean(x * x, axis=-1, keepdims=True)
    return x * lax.rsqrt(ms + EPS) * (nw * (1.0 + scale)) + shift


def _split3(v):
    hi = v.astype(BF16)
    r1 = v - hi.astype(F32)
    mid = r1.astype(BF16)
    lo = (r1 - mid.astype(F32)).astype(BF16)
    return hi, mid, lo


def _dt_lanes(h, wdt_ref, dtb_ref, alog_ref):
    w = wdt_ref[...]
    w_hi = w.astype(BF16)
    w_lo = (w - w_hi.astype(F32)).astype(BF16)
    h_hi = h.astype(BF16)
    h_lo = (h - h_hi.astype(F32)).astype(BF16)
    both = _dot(h_hi, jnp.concatenate([w_hi, w_lo], axis=1))
    raw = both[:, :LANES] + both[:, LANES:] + _dot(h_lo, w_hi)
    v = raw + dtb_ref[...]
    dt = jnp.maximum(v, 0.0) + jnp.log1p(jnp.exp(-jnp.abs(v)))
    lane = lax.broadcasted_iota(jnp.int32, (1, LANES), 1)
    mult = jnp.where((lane % 64) < HEADS, 1.0, -jnp.exp(alog_ref[...]))
    return dt * mult


def _ada_kernel(cond_ref, w_ref, b_ref, o_ref):
    c = cond_ref[...]
    s = _silu(c)
    o_ref[...] = _dot(s.astype(BF16), w_ref[...].astype(BF16)) + b_ref[...]


def _ada(cond, w, b):
    n = w.shape[1]
    tn = 1024
    return pl.pallas_call(
        _ada_kernel,
        grid=(n // tn,),
        in_specs=[
            pl.BlockSpec((16, D), lambda j: (0, 0)),
            pl.BlockSpec((D, tn), lambda j: (0, j)),
            pl.BlockSpec((1, tn), lambda j: (0, j)),
        ],
        out_specs=pl.BlockSpec((16, tn), lambda j: (0, j)),
        out_shape=jax.ShapeDtypeStruct((16, n), F32),
        name="ada",
    )(cond, w, b)


def _conv5_silu(p_scr, rows, step, cw, cb):
    acc = cb
    for k in range(SSM_K):
        acc = acc + p_scr[pl.ds(HALO - (SSM_K // 2) * step + k * step, rows), :] * cw[k:k + 1, :]
    return _silu(acc)


SUB = 8
NSEG_IN = SUB
SEG_IN = TM_IN // NSEG_IN


def _to_seg_major(dst, src, nseg, seg, row0=0):
    for j in range(dst.shape[0]):
        for s in range(nseg):
            dst[j, pl.ds(row0 + s, seg, stride=nseg), :] = src[pl.ds(row0 + s * seg, seg), pl.ds(j * LANES, LANES)]


def _from_seg_major(dst, col0, src, nseg, seg, row0=0):
    for j in range(src.shape[0]):
        for s in range(nseg):
            dst[pl.ds(row0 + s * seg, seg), pl.ds(col0 + j * LANES, LANES)] = (
                src[j, pl.ds(row0 + s, seg, stride=nseg), :].astype(dst.dtype))


def _load_cols(ref):
    return jnp.concatenate([ref[j] for j in range(ref.shape[0])], axis=1)


def _store_cols(ref, v):
    for j in range(ref.shape[0]):
        ref[j] = v[:, j * LANES:(j + 1) * LANES]


def _ssd_in_kernel(xp_ref, x_ref, xn_ref, nw_ref, sh_ref, sc_ref, w_ref, wdt_ref, dtb_ref,
                   alog_ref, cw_ref, cb_ref, xt_ref, bc_ref, dt_ref, xs_scr, h_scr, p_scr, r_scr, v_scr):
    i = pl.program_id(0)
    tpb = SEQ // TM_IN
    first = (i % tpb) == 0
    last = (i % tpb) == tpb - 1
    nw, sh, sc = nw_ref[...], sh_ref[...], sc_ref[...]
    _to_seg_major(xs_scr, x_ref, NSEG_IN, SEG_IN)
    hm = _norm_mod(_load_cols(xs_scr), nw, sh, sc)
    h_scr[pl.ds(0, TM_IN), :] = hm.astype(BF16)
    h_scr[pl.ds(TM_IN, HALO), :] = _norm_mod(xp_ref[...], nw, sh, sc).astype(BF16)
    h_scr[pl.ds(TM_IN + HALO, HALO), :] = _norm_mod(xn_ref[...], nw, sh, sc).astype(BF16)

    val = _dt_lanes(hm, wdt_ref, dtb_ref, alog_ref)
    lane = lax.broadcasted_iota(jnp.int32, (1, LANES), 1)
    v_scr[0] = jnp.where(lane < 64, val, 0.0)
    v_scr[1] = jnp.where(lane < 64, pltpu.roll(val, 64, axis=1), 0.0)
    for d in range(2):
        for s in range(NSEG_IN):
            dt_ref[d, pl.ds(s * SEG_IN, SEG_IN), :] = v_scr[d, pl.ds(s, SEG_IN, stride=NSEG_IN), :]

    sub = lax.broadcasted_iota(jnp.int32, (SUB, CN_IN), 0)
    keep_prev = jnp.where(first, 0.0, 1.0)
    keep_next = jnp.where(last, 0.0, 1.0)
    nblk = TM_IN // SUB
    for c in range(XBC // CN_IN):
        cols = pl.ds(c * CN_IN, CN_IN)
        p = _dot(h_scr[...], w_ref[:, cols])
        pp = p_scr.at[c % 2]
        rr = r_scr.at[c % 2]
        pp[pl.ds(HALO, TM_IN), :] = p[0:TM_IN]
        prev = p[TM_IN + HALO - SUB:TM_IN + HALO] * keep_prev
        nxt = p[TM_IN + HALO:TM_IN + HALO + SUB] * keep_next
        blk = lambda b: p[b * SUB:(b + 1) * SUB]
        pp[pl.ds(HALO - SUB, SUB), :] = jnp.where(
            sub == 0, pltpu.roll(prev, 1, axis=0), pltpu.roll(blk(nblk - 1), 1, axis=0))
        pp[pl.ds(HALO - 2 * SUB, SUB), :] = jnp.where(
            sub == 0, pltpu.roll(prev, 2, axis=0), pltpu.roll(blk(nblk - 2), 1, axis=0))
        pp[pl.ds(HALO + TM_IN, SUB), :] = jnp.where(
            sub == SUB - 1, pltpu.roll(nxt, SUB - 1, axis=0), pltpu.roll(blk(0), SUB - 1, axis=0))
        pp[pl.ds(HALO + TM_IN + SUB, SUB), :] = jnp.where(
            sub == SUB - 1, pltpu.roll(nxt, SUB - 2, axis=0), pltpu.roll(blk(1), SUB - 1, axis=0))
        _store_cols(rr, _conv5_silu(pp, TM_IN, SUB, cw_ref[:, cols], cb_ref[:, cols]))
        if c * CN_IN >= INNER:
            _from_seg_major(bc_ref, c * CN_IN - INNER, rr, NSEG_IN, SEG_IN)
        else:
            for j in range(CN_IN // LANES):
                for m in range(TM_IN // LANES):
                    tok = jnp.concatenate(
                        [rr[j, pl.ds(m * (LANES // SEG_IN) + e, SEG_IN, stride=NSEG_IN), :]
                         for e in range(LANES // SEG_IN)],
                        axis=0)
                    xt_ref[(m * LANES) // QC, pl.ds(c * CN_IN + j * LANES, LANES),
                           pl.ds((m * LANES) % QC, LANES)] = tok.T.astype(BF16)


def _ssd_in(x2, mod3, nw, w_xbc, wdt, dtb, alog, cw, cb):
    n = x2.shape[0]
    nt = n // TM_IN
    tpb = SEQ // TM_IN
    hb = TM_IN // HALO
    nhb = n // HALO
    const = lambda i: (0, 0)
    return pl.pallas_call(
        _ssd_in_kernel,
        grid=(nt,),
        in_specs=[
            pl.BlockSpec((HALO, D), lambda i: (jnp.maximum(i * hb - 1, 0), 0)),
            pl.BlockSpec((TM_IN, D), lambda i: (i, 0)),
            pl.BlockSpec((HALO, D), lambda i: (jnp.minimum((i + 1) * hb, nhb - 1), 0)),
            pl.BlockSpec((1, D), const),
            pl.BlockSpec((None, 1, D), lambda i: (i // tpb, 0, 0)),
            pl.BlockSpec((None, 1, D), lambda i: (i // tpb, 0, 1)),
            pl.BlockSpec((D, XBC), const, pipeline_mode=pl.Buffered(1)),
            pl.BlockSpec((D, LANES), const),
            pl.BlockSpec((1, LANES), const),
            pl.BlockSpec((1, LANES), const),
            pl.BlockSpec((SSM_K, XBC), const),
            pl.BlockSpec((1, XBC), const),
        ],
        out_specs=[
            pl.BlockSpec((TM_IN // QC, INNER, QC), lambda i: (i, 0, 0)),
            pl.BlockSpec((TM_IN, 2 * GN), lambda i: (i, 0)),
            pl.BlockSpec((2, TM_IN, LANES), lambda i: (0, i, 0)),
        ],
        out_shape=[
            jax.ShapeDtypeStruct((n // QC, INNER, QC), BF16),
            jax.ShapeDtypeStruct((n, 2 * GN), BF16),
            jax.ShapeDtypeStruct((2, n, LANES), F32),
        ],
        scratch_shapes=[
            pltpu.VMEM((D // LANES, TM_IN, LANES), F32),
            pltpu.VMEM((TM_IN + 2 * HALO, D), BF16),
            pltpu.VMEM((2, TM_IN + 2 * HALO, CN_IN), F32),
            pltpu.VMEM((2, CN_IN // LANES, TM_IN, LANES), F32),
            pltpu.VMEM((2, TM_IN, LANES), F32),
        ],
        compiler_params=pltpu.CompilerParams(vmem_limit_bytes=VMEM_LIMIT),
        name="ssd_in",
    )(x2, x2, x2, nw, mod3, mod3, w_xbc, wdt, dtb, alog, cw, cb)


CTX_COLS = INNER + GN


def _ctx_kernel(c_ref, nw_ref, sh_ref, sc_ref, w_ref, wdt_ref, dtb_ref, alog_ref, cw_ref, cb_ref,
                h0_ref, p_scr, xb_scr):
    hc = _norm_mod(c_ref[...], nw_ref[...], sh_ref[...], sc_ref[...])
    hb = hc.astype(BF16)
    p_scr[pl.ds(0, HALO), :] = jnp.zeros((HALO, CN_IN), F32)
    p_scr[pl.ds(HALO + CTX, HALO), :] = jnp.zeros((HALO, CN_IN), F32)
    for c in range(CTX_COLS // CN_IN):
        cols = pl.ds(c * CN_IN, CN_IN)
        p_scr[pl.ds(HALO, CTX), :] = _dot(hb, w_ref[:, cols])
        xb_scr[:, cols] = _conv5_silu(p_scr, CTX, 1, cw_ref[:, cols], cb_ref[:, cols]).astype(BF16)

    val = _dt_lanes(hc, wdt_ref, dtb_ref, alog_ref)
    row = lax.broadcasted_iota(jnp.int32, (CTX, CTX), 0)
    col = lax.broadcasted_iota(jnp.int32, (CTX, CTX), 1)
    tri = (row >= col).astype(BF16)
    v_hi, v_mid, v_lo = _split3(val)
    cum = _dot(tri, v_hi) + _dot(tri, v_mid) + _dot(tri, v_lo)
    tot = cum[CTX - 1:CTX, :]
    dts = pltpu.roll(val, HEADS, axis=1)
    lane = lax.broadcasted_iota(jnp.int32, (1, LANES), 1)
    wgt = jnp.where(lane < 64, jnp.exp(tot - cum), jnp.exp(cum - val)) * dts
    lane2 = lax.broadcasted_iota(jnp.int32, (CTX, LANES), 1)
    for d in range(2):
        base = d * 64 + HEADS
        for pair in range(HEADS // 2):
            g = pair // 2
            la = base + 2 * pair
            wcol = jnp.where(lane2 < HEAD_DIM, wgt[:, la:la + 1], wgt[:, la + 1:la + 2])
            xs = (xb_scr[:, pl.ds(pair * LANES, LANES)].astype(F32) * wcol).astype(BF16)
            bg = xb_scr[:, pl.ds(INNER + g * STATE, STATE)]
            h0_ref[d, pl.ds(pair * LANES, LANES), :] = _dot_tn(xs, bg)


def _ctx(ctx, mod3, nw, w_xbc, wdt, dtb, alog, cw, cb):
    b = ctx.shape[0]
    const = lambda i: (0, 0)
    return pl.pallas_call(
        _ctx_kernel,
        grid=(b,),
        in_specs=[
            pl.BlockSpec((None, CTX, D), lambda i: (i, 0, 0)),
            pl.BlockSpec((1, D), const),
            pl.BlockSpec((None, 1, D), lambda i: (8, 0, 0)),
            pl.BlockSpec((None, 1, D), lambda i: (8, 0, 1)),
            pl.BlockSpec((D, CTX_COLS), const),
            pl.BlockSpec((D, LANES), const),
            pl.BlockSpec((1, LANES), const),
            pl.BlockSpec((1, LANES), const),
            pl.BlockSpec((SSM_K, CTX_COLS), const),
            pl.BlockSpec((1, CTX_COLS), const),
        ],
        out_specs=pl.BlockSpec((2, None, INNER, STATE), lambda i: (0, i, 0, 0)),
        out_shape=jax.ShapeDtypeStruct((2, b, INNER, STATE), F32),
        scratch_shapes=[
            pltpu.VMEM((CTX + 2 * HALO, CN_IN), F32),
            pltpu.VMEM((CTX, CTX_COLS), BF16),
        ],
        compiler_params=pltpu.CompilerParams(vmem_limit_bytes=VMEM_LIMIT),
        name="ctx",
    )(ctx, nw, mod3, mod3, w_xbc, wdt, dtb, alog, cw, cb)


def _scan_dir(d_smem, xt_ref, b_ref, c_ref, dt_ref, y_ref, st, bwd):
    q = QC
    hq = q // 2
    dtb = dt_ref[...]
    row = lax.broadcasted_iota(jnp.int32, (q, q), 0)
    col = lax.broadcasted_iota(jnp.int32, (q, q), 1)
    tri = ((col >= row) if bwd else (row >= col)).astype(BF16)
    v_hi, v_mid, v_lo = _split3(dtb)
    cum = (_dot(tri, v_hi) + _dot(tri, v_mid) + _dot(tri, v_lo)) * LOG2E
    cum_t = cum.T[HEADS:2 * HEADS, :]
    dt_t = dtb.T[0:HEADS, :]
    tot_t = cum_t[:, 0:1] if bwd else cum_t[:, q - 1:q]
    ecum_t = jnp.exp2(cum_t)
    dte_t = jnp.exp2(tot_t - cum_t) * dt_t
    etot = jnp.exp2(tot_t)
    keep = ((row >= col) if bwd else (col >= row)).astype(F32)
    fr = pl.ds(hq, hq) if bwd else pl.ds(0, hq)
    pr = pl.ds(0, hq) if bwd else pl.ds(hq, hq)
    fs = slice(hq, q) if bwd else slice(0, hq)
    ps = slice(0, hq) if bwd else slice(hq, q)
    zero = jnp.zeros((hq, hq), BF16)

    gp = (HEADS // GROUPS) * HEAD_DIM
    hpg = HEADS // GROUPS

    def prepare(g):
        cg = c_ref[:, pl.ds(g * STATE, STATE)]
        s_old = st[pl.ds(g * gp, gp), :]
        both = _dot_nt(jnp.concatenate([b_ref[fr, pl.ds(g * STATE, STATE)], s_old.astype(BF16)], axis=0), cg)
        cb_diag = (both[0:hq, fs] * keep[fs, fs]).astype(BF16)
        cb_off = both[0:hq, ps].astype(BF16)
        cbp = (_dot_nt(b_ref[pr, pl.ds(g * STATE, STATE)], c_ref[pr, pl.ds(g * STATE, STATE)])
               * keep[ps, ps]).astype(BF16)
        mts = []
        for j in range(hpg):
            h = g * hpg + j
            crow = cum_t[h:h + 1, :]
            ccol = cum[:, HEADS + h:HEADS + h + 1]
            m_diag = cb_diag * jnp.exp2(jnp.minimum(crow[:, fs] - ccol[fs], 0.0)).astype(BF16)
            m_off = cb_off * jnp.exp2(crow[:, ps] - ccol[fs]).astype(BF16)
            mp = cbp * jnp.exp2(jnp.minimum(crow[:, ps] - ccol[ps], 0.0)).astype(BF16)
            if bwd:
                mts.append(jnp.concatenate([jnp.concatenate([mp, zero], axis=1),
                                            jnp.concatenate([m_off, m_diag], axis=1)], axis=0))
            else:
                mts.append(jnp.concatenate([jnp.concatenate([m_diag, m_off], axis=1),
                                            jnp.concatenate([zero, mp], axis=1)], axis=0))
        return s_old, both[hq:hq + gp], mts

    def finish(g, s_old, y_off, mts):
        xs, cds = [], []
        for j in range(hpg):
            h = g * hpg + j
            r0 = g * gp + j * HEAD_DIM
            xb = xt_ref[pl.ds(r0, HEAD_DIM), :]
            y_diag = _dot(xb * jnp.broadcast_to(dt_t[h:h + 1, :], xb.shape).astype(BF16), mts[j])
            yh = y_diag + y_off[j * HEAD_DIM:(j + 1) * HEAD_DIM] * ecum_t[h:h + 1, :]
            if not bwd:
                yh = yh + d_smem[h] * xb.astype(F32)
            y_ref[pl.ds(r0, HEAD_DIM), :] = yh.astype(BF16)
            xs.append(xb * jnp.broadcast_to(dte_t[h:h + 1, :], xb.shape).astype(BF16))
            cds.append(jnp.broadcast_to(etot[h:h + 1, :], (HEAD_DIM, STATE)))
        s_new = _dot(jnp.concatenate(xs, axis=0), b_ref[:, pl.ds(g * STATE, STATE)])
        st[pl.ds(g * gp, gp), :] = s_old * jnp.concatenate(cds, axis=0) + s_new

    return prepare, finish


def _scan_kernel(d_smem, xtf_ref, bf_ref, cf_ref, dtf_ref, h0f_ref, xtb_ref, bb_ref, cb_ref, dtb_ref, h0b_ref,
                 yf_ref, yb_ref, stf, stb):
    @pl.when(pl.program_id(1) == 0)
    def _():
        stf[...] = h0f_ref[...]
        stb[...] = h0b_ref[...]

    dirs = [_scan_dir(d_smem, xtf_ref, bf_ref, cf_ref, dtf_ref, yf_ref, stf, False),
            _scan_dir(d_smem, xtb_ref, bb_ref, cb_ref, dtb_ref, yb_ref, stb, True)]
    nxt = [prepare(0) for prepare, _ in dirs]
    for g in range(GROUPS):
        cur = nxt
        if g + 1 < GROUPS:
            nxt = [prepare(g + 1) for prepare, _ in dirs]
        for (_, finish), args in zip(dirs, cur):
            finish(g, *args)


def _scan(ssm_d, xt, bc, dtda, h0):
    n = bc.shape[0]
    bsz = n // SEQ
    nc = SEQ // QC
    fwd = lambda b, c: b * nc + c
    bwd = lambda b, c: b * nc + nc - 1 - c

    def specs(d, chunk):
        return [
            pl.BlockSpec((None, INNER, QC), lambda b, c: (chunk(b, c), 0, 0)),
            pl.BlockSpec((QC, GN), lambda b, c: (chunk(b, c), 0)),
            pl.BlockSpec((QC, GN), lambda b, c: (chunk(b, c), 1)),
            pl.BlockSpec((None, QC, LANES), lambda b, c: (d, chunk(b, c), 0)),
            pl.BlockSpec((None, None, INNER, STATE), lambda b, c: (d, b, 0, 0)),
        ]

    y_shape = jax.ShapeDtypeStruct((n // QC, INNER, QC), BF16)
    return pl.pallas_call(
        _scan_kernel,
        grid=(bsz, nc),
        in_specs=[pl.BlockSpec(memory_space=pltpu.SMEM)] + specs(0, fwd) + specs(1, bwd),
        out_specs=[pl.BlockSpec((None, INNER, QC), lambda b, c: (fwd(b, c), 0, 0)),
                   pl.BlockSpec((None, INNER, QC), lambda b, c: (bwd(b, c), 0, 0))],
        out_shape=[y_shape, y_shape],
        scratch_shapes=[pltpu.VMEM((INNER, STATE), F32), pltpu.VMEM((INNER, STATE), F32)],
        compiler_params=pltpu.CompilerParams(
            dimension_semantics=("arbitrary", "arbitrary"),
            vmem_limit_bytes=VMEM_LIMIT),
        name="scan",
    )(ssm_d, xt, bc, bc, dtda, h0, xt, bc, bc, dtda, h0)


NSEG_CONV = SUB
HALF_CONV = NSEG_CONV * GRID_W
assert TM_CONV % HALF_CONV == 0
RB_CONV = 64
MM_CONV = 256
TAPS_CONV = 8


def _conv_kernel(x_ref, nw_ref, sh_ref, sc_ref, wglu_ref, wg_ref, dw_ref, db_ref, lnw_ref, lnb_ref,
                 wco_ref, bco_ref, a_ref, xs_scr, pad_scr, u_scr):
    halves = [hf * HALF_CONV for hf in range(TM_CONV // HALF_CONV)]
    for r_h in halves:
        _to_seg_major(xs_scr, x_ref, NSEG_CONV, GRID_W, r_h)
    h = _norm_mod(_load_cols(xs_scr), nw_ref[...], sh_ref[...], sc_ref[...]).astype(BF16)
    per_mm = MM_CONV // LANES

    def glu(cb):
        val = _dot(h, wglu_ref[:, pl.ds(cb * MM_CONV, MM_CONV)])
        gate = _dot(h, wglu_ref[:, pl.ds(D + cb * MM_CONV, MM_CONV)])
        return val * _sigmoid(gate)

    for cb in range(D // MM_CONV):
        u = glu(cb)
        for j in range(per_mm):
            cc = cb * per_mm + j
            pad_scr[cc] = u[:, j * LANES:(j + 1) * LANES]
            nout = RB_CONV // SUB
            for r_h, rb in [(r_h, rb) for r_h in halves for rb in range(HALF_CONV // RB_CONV)]:
                acc = [db_ref[cc]] * nout
                for k0 in range(0, CONV_K, TAPS_CONV):
                    nk = min(TAPS_CONV, CONV_K - k0)
                    base = rb * nout + k0 - CONV_K // 2
                    src = {i: pad_scr[cc, pl.ds(r_h + (base + i) * SUB, SUB), :]
                           for i in range(nout + nk - 1) if 0 <= base + i < GRID_W}
                    for kk in range(nk):
                        w = dw_ref[cc, k0 + kk]
                        for o in range(nout):
                            if o + kk in src:
                                acc[o] = acc[o] + src[o + kk] * w
                for o in range(nout):
                    u_scr[cc, pl.ds(r_h + (rb * nout + o) * SUB, SUB), :] = acc[o]

    v = _load_cols(u_scr)
    mu = jnp.mean(v, axis=-1, keepdims=True)
    vc = v - mu
    var = jnp.mean(vc * vc, axis=-1, keepdims=True)
    ln = vc * lax.rsqrt(var + EPS) * lnw_ref[...] + lnb_ref[...]
    act = _silu(ln).astype(BF16)
    u_conv = _dot(act, wco_ref[...]) + bco_ref[...]
    g_conv = _sigmoid(_dot(h, wg_ref[...]))
    _store_cols(u_scr, g_conv * u_conv)
    for r_h in halves:
        _from_seg_major(a_ref, 0, u_scr, NSEG_CONV, GRID_W, r_h)


def _conv_branch(x2, mod3, nw, w_rest, dw, db, lnw, lnb, wco, bco):
    n = x2.shape[0]
    tpb = SEQ // TM_CONV
    const = lambda i: (0, 0)
    return pl.pallas_call(
        _conv_kernel,
        grid=(n // TM_CONV,),
        in_specs=[
            pl.BlockSpec((TM_CONV, D), lambda i: (i, 0)),
            pl.BlockSpec((1, D), const),
            pl.BlockSpec((None, 1, D), lambda i: (i // tpb, 0, 0)),
            pl.BlockSpec((None, 1, D), lambda i: (i // tpb, 0, 1)),
            pl.BlockSpec((D, 2 * D), lambda i: (0, (COL_GLU - COL_Z) // (2 * D))),
            pl.BlockSpec((D, D), lambda i: (0, (COL_GATE - COL_Z) // D)),
            pl.BlockSpec((D // LANES, CONV_K, SUB, LANES), lambda i: (0, 0, 0, 0)),
            pl.BlockSpec((D // LANES, SUB, LANES), lambda i: (0, 0, 0)),
            pl.BlockSpec((1, D), const),
            pl.BlockSpec((1, D), const),
            pl.BlockSpec((D, D), const),
            pl.BlockSpec((1, D), const),
        ],
        out_specs=pl.BlockSpec((TM_CONV, D), lambda i: (i, 0)),
        out_shape=jax.ShapeDtypeStruct((n, D), BF16),
        scratch_shapes=[
            pltpu.VMEM((D // LANES, TM_CONV, LANES), F32),
            pltpu.VMEM((D // LANES, TM_CONV, LANES), F32),
            pltpu.VMEM((D // LANES, TM_CONV, LANES), F32),
        ],
        compiler_params=pltpu.CompilerParams(vmem_limit_bytes=VMEM_LIMIT),
        name="conv",
    )(x2, nw, mod3, mod3, w_rest, w_rest, dw, db, lnw, lnb, wco, bco)


def _merge_kernel(x_ref, yf_ref, yb_ref, a_ref, nw_ref, sh_ref, sc_ref, gt_ref, wz_ref, wg_ref,
                  snw_ref, wso_ref, wo_ref, o_ref):
    x = x_ref[...]
    h = _norm_mod(x, nw_ref[...], sh_ref[...], sc_ref[...]).astype(BF16)
    z = _dot(h, wz_ref[...])
    y = jnp.concatenate([(yf_ref[k].astype(F32) + yb_ref[k].astype(F32)).T for k in range(TM_MERGE // QC)], axis=0)
    y = y * _silu(z)
    ms = jnp.mean(y * y, axis=-1, keepdims=True)
    yn = (y * lax.rsqrt(ms + EPS) * snw_ref[...]).astype(BF16)
    u_ssd = _dot(yn, wso_ref[...])
    g_ssd = _sigmoid(_dot(h, wg_ref[...]))
    m = a_ref[...].astype(F32) + g_ssd * u_ssd
    mix = _dot(m.astype(BF16), wo_ref[...])
    o_ref[...] = x + gt_ref[...] * mix


def _merge(x2, yf, yb, a, mod3, nw, w_rest, snw, wso, wo):
    n = x2.shape[0]
    tm = TM_MERGE
    assert tm % QC == 0
    tpb = SEQ // tm
    const = lambda i: (0, 0)
    return pl.pallas_call(
        _merge_kernel,
        grid=(n // tm,),
        in_specs=[
            pl.BlockSpec((tm, D), lambda i: (i, 0)),
            pl.BlockSpec((tm // QC, INNER, QC), lambda i: (i, 0, 0)),
            pl.BlockSpec((tm // QC, INNER, QC), lambda i: (i, 0, 0)),
            pl.BlockSpec((tm, D), lambda i: (i, 0)),
            pl.BlockSpec((1, D), const),
            pl.BlockSpec((None, 1, D), lambda i: (i // tpb, 0, 0)),
            pl.BlockSpec((None, 1, D), lambda i: (i // tpb, 0, 1)),
            pl.BlockSpec((None, 1, D), lambda i: (i // tpb, 0, 2)),
            pl.BlockSpec((D, INNER), const),
            pl.BlockSpec((D, D), lambda i: (0, (COL_GATE - COL_Z) // D + 1)),
            pl.BlockSpec((1, INNER), const),
            pl.BlockSpec((INNER, D), const),
            pl.BlockSpec((D, D), const),
        ],
        out_specs=pl.BlockSpec((tm, D), lambda i: (i, 0)),
        out_shape=jax.ShapeDtypeStruct((n, D), F32),
        compiler_params=pltpu.CompilerParams(vmem_limit_bytes=VMEM_LIMIT),
        name="merge",
    )(x2, yf, yb, a, nw, mod3, mod3, mod3, w_rest, w_rest, snw, wso, wo)


def _mlp_kernel(x_ref, nw_ref, sh_ref, sc_ref, gt_ref, w1_ref, w2_ref, fnw_ref, o_ref):
    x = x_ref[...]
    h = _norm_mod(x, nw_ref[...], sh_ref[...], sc_ref[...]).astype(BF16)
    acc = jnp.zeros((TM_MLP, D), F32)
    for c in range(D_FF // FF_CHUNK):
        t = jnp.maximum(_dot(h, w1_ref[:, pl.ds(c * FF_CHUNK, FF_CHUNK)]), 0.0)
        acc = acc + _dot((t * t).astype(BF16), w2_ref[pl.ds(c * FF_CHUNK, FF_CHUNK), :])
    x2 = x + gt_ref[...] * acc
    ms = jnp.mean(x2 * x2, axis=-1, keepdims=True)
    o_ref[...] = x2 * lax.rsqrt(ms + EPS) * fnw_ref[...]


def _mlp(x1, mod3, nw, w1, w2, fnw):
    n = x1.shape[0]
    tm = TM_MLP
    tpb = SEQ // tm
    const = lambda i: (0, 0)
    return pl.pallas_call(
        _mlp_kernel,
        grid=(n // tm,),
        in_specs=[
            pl.BlockSpec((tm, D), lambda i: (i, 0)),
            pl.BlockSpec((1, D), const),
            pl.BlockSpec((None, 1, D), lambda i: (i // tpb, 0, 3)),
            pl.BlockSpec((None, 1, D), lambda i: (i // tpb, 0, 4)),
            pl.BlockSpec((None, 1, D), lambda i: (i // tpb, 0, 5)),
            pl.BlockSpec((D, D_FF), const, pipeline_mode=pl.Buffered(1)),
            pl.BlockSpec((D_FF, D), const, pipeline_mode=pl.Buffered(1)),
            pl.BlockSpec((1, D), const),
        ],
        out_specs=pl.BlockSpec((tm, D), lambda i: (i, 0)),
        out_shape=jax.ShapeDtypeStruct((n, D), F32),
        compiler_params=pltpu.CompilerParams(vmem_limit_bytes=VMEM_LIMIT),
        name="mlp",
    )(x1, nw, mod3, mod3, mod3, w1, w2, fnw)


def kernel(x, c, ctx, c_ctx, w_ada, b_ada, norm1_w, norm2_w, w_in, conv_dw_w, conv_dw_b, conv_ln_w,
           conv_ln_b, w_conv_out, b_conv_out, ssm_conv_w, ssm_conv_b, ssm_dt_bias, ssm_a_log, ssm_d,
           ssm_norm_w, w_ssm_out, w_o, w_mlp1, w_mlp2, final_norm_w):
    bsz, seq, _ = x.shape
    assert (bsz, seq, x.shape[2]) == (8, SEQ, D) and ctx.shape[1] == CTX and w_ada.shape[0] == 1
    n = bsz * seq
    x2 = x.reshape(n, D)
    row = lambda v: v.reshape(1, -1)

    wi = w_in[0]
    w_xbc = wi[:, :COL_DT].astype(BF16)
    wdt_f = wi[:, COL_DT:COL_DT + HEADS]
    wdt_b = wi[:, COL_DT + HEADS:COL_Z]
    wdt = jnp.concatenate([wdt_f, wdt_f, wdt_b, wdt_b], axis=1)
    bf_, bb_ = ssm_dt_bias[0, 0], ssm_dt_bias[0, 1]
    dtb = row(jnp.concatenate([bf_, bf_, bb_, bb_]))
    zh = jnp.zeros((HEADS,), F32)
    alog = row(jnp.concatenate([zh, ssm_a_log[0, 0], zh, ssm_a_log[0, 1]]))
    w_rest = wi[:, COL_Z:].astype(BF16)

    cond = jnp.concatenate([c, c_ctx[None, :], jnp.zeros((16 - bsz - 1, D), F32)], axis=0)
    mod = _ada(cond, w_ada[0], row(b_ada[0]))
    mod3 = mod.reshape(16, 1, 6 * D)

    nw1 = row(norm1_w[0])
    cw, cb = ssm_conv_w[0], row(ssm_conv_b[0])
    h0 = _ctx(ctx, mod3, nw1, w_xbc, wdt, dtb, alog, cw, cb)
    xt, bc, dtda = _ssd_in(x2, mod3, nw1, w_xbc, wdt, dtb, alog, cw, cb)
    yf, yb = _scan(ssm_d[0], xt, bc, dtda, h0)
    dw3 = jnp.broadcast_to(conv_dw_w[0].reshape(CONV_K, D // LANES, 1, LANES).transpose(1, 0, 2, 3),
                           (D // LANES, CONV_K, SUB, LANES))
    db3 = jnp.broadcast_to(conv_dw_b[0].reshape(D // LANES, 1, LANES), (D // LANES, SUB, LANES))
    a = _conv_branch(x2, mod3, nw1, w_rest, dw3, db3, row(conv_ln_w[0]),
                     row(conv_ln_b[0]), w_conv_out[0].astype(BF16), row(b_conv_out[0]))
    x1 = _merge(x2, yf, yb, a, mod3, nw1, w_rest, row(ssm_norm_w[0]),
                w_ssm_out[0].astype(BF16), w_o[0].astype(BF16))
    out = _mlp(x1, mod3, row(norm2_w[0]), w_mlp1[0].astype(BF16), w_mlp2[0].astype(BF16),
               row(final_norm_w))
    return out.reshape(bsz, seq, D)
```
